```python
import jax, jax.numpy as jnp
from jax import lax
import numpy as np

D_MODEL = 1024
BATCH = 32
SEQ = 2048
DEPTH = 2

GRID_W = 64
CTX_LEN = 256
N_MIXERS = 2
N_HGRN = (DEPTH + N_MIXERS - 1) // N_MIXERS
N_CMLP = DEPTH // N_MIXERS
HG_HEADS = 8
HG_DK = D_MODEL // HG_HEADS
HG_DV = D_MODEL // HG_HEADS
HG_CHUNK = 64
CM_CHUNK = 128
CM_GROUPS = 8
CM_INNER = 3 * D_MODEL
FFN_HIDDEN = ((8 * D_MODEL // 3 + 127) // 128) * 128
DN_ALPHA = (2 * DEPTH) ** 0.25
DN_BETA = (8 * DEPTH) ** -0.25
LN_EPS = 1e-5
RMS_EPS = 1e-6
N_MOD = 9

kernel_name = "hybrid_hgrn2_chunkmlp_flow_block"


def layer_norm(x, g, b):
    xf = x.astype(jnp.float32)
    mu = jnp.mean(xf, axis=-1, keepdims=True)
    var = jnp.mean(jnp.square(xf - mu), axis=-1, keepdims=True)
    return ((xf - mu) * lax.rsqrt(var + LN_EPS) * g + b).astype(x.dtype)


def modulation(cond, w_mod, b_mod):
    m = jax.nn.silu(cond) @ w_mod + b_mod
    return m.reshape(m.shape[:-1] + (N_MOD, D_MODEL))


def modulate(x, m, k):
    return x * (1 + m[..., 3 * k + 1, :]) + m[..., 3 * k, :]


def post_norm_update(x, y, m, k, g, b):
    return layer_norm(DN_ALPHA * x + m[..., 3 * k + 2, :] * y, g, b)


def swiglu(h, w_in, w_out):
    gate, up = jnp.split(h @ w_in, 2, axis=-1)
    return (jax.nn.silu(gate) * up) @ w_out


def macaron_half_ffn(x, m, k, w_in, w_out, g, b):
    return post_norm_update(x, 0.5 * swiglu(modulate(x, m, k), w_in, w_out), m, k, g, b)


def gla_chunkwise(q, k, v, logf, s0):
    bsz, nh, length, dk = q.shape
    dv = v.shape[-1]
    n = length // HG_CHUNK

    def blocks(t):
        return t.reshape(bsz, nh, n, HG_CHUNK, t.shape[-1]).transpose(2, 0, 1, 3, 4)

    q, k, v, logf = blocks(q), blocks(k), blocks(v), blocks(logf)
    b = jnp.cumsum(logf, axis=-2)
    b_last = b[..., -1:, :]
    q_in = q * jnp.exp(b)
    k_in = k * jnp.exp(-b)
    k_out = k * jnp.exp(b_last - b)
    lower = jnp.tril(jnp.ones((HG_CHUNK, HG_CHUNK), dtype=bool))
    scores = jnp.where(lower, jnp.einsum('nbhtd,nbhsd->nbhts', q_in, k_in), 0.0)
    o_intra = jnp.einsum('nbhts,nbhse->nbhte', scores, v)
    kv_chunk = jnp.einsum('nbhsd,nbhse->nbhde', k_out, v)
    decay_chunk = jnp.exp(b_last[..., 0, :])

    def step(state, xs):
        q_c, dec_c, kv_c = xs
        o_c = jnp.einsum('bhtd,bhde->bhte', q_c, state)
        return dec_c[..., None] * state + kv_c, o_c

    s_final, o_inter = lax.scan(step, s0, (q_in, decay_chunk, kv_chunk))
    o = (o_intra + o_inter).transpose(1, 2, 0, 3, 4).reshape(bsz, nh, length, dv)
    return o, s_final


def hgrn2_project(h, w_in, lb):
    bsz, length, _ = h.shape
    p = (h @ w_in).astype(jnp.float32)
    p = p.reshape(bsz, length, 5, HG_HEADS, HG_DK).transpose(2, 0, 3, 1, 4)
    q = jax.nn.silu(p[0])
    i = p[1]
    lbb = lb[:, None]
    f = lbb + (1 - lbb) * jax.nn.sigmoid(p[2:4])
    return q, i, 1 - f, jnp.log(f), p[4]


def hgrn2_readout(o, g, norm_w, w_out):
    bsz, nh, length, dv = o.shape
    o = o * lax.rsqrt(jnp.mean(o * o, axis=-1, keepdims=True) + RMS_EPS) * norm_w
    o = o * jax.nn.silu(g)
    return o.transpose(0, 2, 1, 3).reshape(bsz, length, nh * dv).astype(w_out.dtype) @ w_out


def hgrn2_mixer(hx, hc, w_in, lb, norm_w, w_out, ctx_out):
    qx, ix, kx, lfx, gx = hgrn2_project(hx, w_in, lb)
    qc, ic, kc, lfc, gc = hgrn2_project(hc, w_in, lb)
    s0 = jnp.zeros((hc.shape[0], HG_HEADS, HG_DK, HG_DV), jnp.float32)
    rev = lambda t: jnp.flip(t, axis=2)
    oc_f, sc_f = gla_chunkwise(qc, kc[0], ic, lfc[0], s0)
    oc_b, sc_b = gla_chunkwise(rev(qc), rev(kc[1]), rev(ic), rev(lfc[1]), s0)
    ox_f, _ = gla_chunkwise(qx, kx[0], ix, lfx[0], sc_f)
    ox_b, _ = gla_chunkwise(rev(qx), rev(kx[1]), rev(ix), rev(lfx[1]), sc_b)
    yx = hgrn2_readout(ox_f + rev(ox_b), gx, norm_w, w_out)
    yc = hgrn2_readout(oc_f + rev(oc_b), gc, norm_w, w_out) if ctx_out else None
    return yx, yc


def chunk_mlp(h, n_chunks, w_in, v_g, v_b, w_s, b_s, w_out):
    bsz, length, _ = h.shape
    u, v = jnp.split(jax.nn.gelu(h @ w_in), 2, axis=-1)
    v = layer_norm(v, v_g, v_b)
    v = v.reshape(bsz, n_chunks, CM_CHUNK, CM_GROUPS, CM_INNER // CM_GROUPS)
    v = jnp.einsum('gts,bnsgc->bntgc', w_s, v) + b_s.T[None, None, :, :, None]
    return (u * v.reshape(bsz, length, CM_INNER)) @ w_out


def setup_inputs(seed: int = 0) -> dict:
    key = jax.random.key(seed)
    ks = jax.random.split(key, 24)
    nrm = lambda k, shape, s: jax.random.normal(k, shape, jnp.float32) * s
    d = D_MODEL
    return {
        "x": nrm(ks[0], (BATCH, SEQ, d), 1.0),
        "c": nrm(ks[1], (BATCH, d), 1.0),
        "ctx": nrm(ks[2], (BATCH, CTX_LEN, d), 1.0),
        "c_ctx": nrm(ks[3], (d,), 1.0),
        "mod_w": nrm(ks[4], (DEPTH, d, N_MOD * d), 0.5 * d ** -0.5),
        "mod_b": nrm(ks[5], (DEPTH, N_MOD * d), 0.02),
        "ln_g": 1.0 + nrm(ks[6], (DEPTH, 3, d), 0.02),
        "ln_b": nrm(ks[7], (DEPTH, 3, d), 0.02),
        "ffn_w_in": nrm(ks[8], (DEPTH, 2, d, 2 * FFN_HIDDEN), d ** -0.5),
        "ffn_w_out": nrm(ks[9], (DEPTH, 2, FFN_HIDDEN, d), DN_BETA * FFN_HIDDEN ** -0.5),
        "hg_w_in": nrm(ks[10], (N_HGRN, d, 5 * d), d ** -0.5),
        "hg_lower_bounds": nrm(ks[11], (2, DEPTH + 1, d), 0.1),
        "hg_norm_w": 1.0 + nrm(ks[12], (N_HGRN, HG_DV), 0.02),
        "hg_w_out": nrm(ks[13], (N_HGRN, d, d), DN_BETA * d ** -0.5),
        "cm_w_in": nrm(ks[14], (N_CMLP, d, 2 * CM_INNER), d ** -0.5),
        "cm_v_g": 1.0 + nrm(ks[15], (N_CMLP, CM_INNER), 0.02),
        "cm_v_b": nrm(ks[16], (N_CMLP, CM_INNER), 0.02),
        "cm_w_s": nrm(ks[17], (N_CMLP, CM_GROUPS, CM_CHUNK, CM_CHUNK), CM_CHUNK ** -0.5),
        "cm_b_s": 1.0 + nrm(ks[18], (N_CMLP, CM_GROUPS, CM_CHUNK), 0.02),
        "cm_w_out": nrm(ks[19], (N_CMLP, CM_INNER, d), DN_BETA * CM_INNER ** -0.5),
    }


def reference(x, c, ctx, c_ctx, mod_w, mod_b, ln_g, ln_b, ffn_w_in, ffn_w_out,
              hg_w_in, hg_lower_bounds, hg_norm_w, hg_w_out,
              cm_w_in, cm_v_g, cm_v_b, cm_w_s, cm_b_s, cm_w_out):
    rows = x.shape[1] // GRID_W
    n_lat_chunks = rows // (CM_CHUNK // GRID_W)
    n_ctx_chunks = ctx.shape[1] // CM_CHUNK
    lb_all = jnp.cumsum(jax.nn.softmax(hg_lower_bounds.astype(jnp.float32), axis=1), axis=1)
    for i in range(DEPTH):
        last = i == DEPTH - 1
        kind = i % N_MIXERS
        j = i // N_MIXERS
        ctx_needed = (not last) or kind == 0
        mx = modulation(c, mod_w[i], mod_b[i])[:, None]
        x = macaron_half_ffn(x, mx, 0, ffn_w_in[i, 0], ffn_w_out[i, 0], ln_g[i, 0], ln_b[i, 0])
        if ctx_needed:
            mc = modulation(c_ctx, mod_w[i], mod_b[i])
            ctx = macaron_half_ffn(ctx, mc, 0, ffn_w_in[i, 0], ffn_w_out[i, 0], ln_g[i, 0], ln_b[i, 0])
        hx = modulate(x, mx, 1)
        if kind == 0:
            lb = lb_all[:, i].reshape(2, HG_HEADS, 1, HG_DK)
            yx, yc = hgrn2_mixer(hx, modulate(ctx, mc, 1), hg_w_in[j], lb, hg_norm_w[j], hg_w_out[j],
                                 ctx_out=not last)
        else:
            yx = chunk_mlp(hx, n_lat_chunks, cm_w_in[j], cm_v_g[j], cm_v_b[j], cm_w_s[j], cm_b_s[j], cm_w_out[j])
            yc = None if last else chunk_mlp(modulate(ctx, mc, 1), n_ctx_chunks, cm_w_in[j], cm_v_g[j],
                                             cm_v_b[j], cm_w_s[j], cm_b_s[j], cm_w_out[j])
        x = post_norm_update(x, yx, mx, 1, ln_g[i, 1], ln_b[i, 1])
        x = macaron_half_ffn(x, mx, 2, ffn_w_in[i, 1], ffn_w_out[i, 1], ln_g[i, 2], ln_b[i, 2])
        if not last:
            ctx = post_norm_update(ctx, yc, mc, 1, ln_g[i, 1], ln_b[i, 1])
            ctx = macaron_half_ffn(ctx, mc, 2, ffn_w_in[i, 1], ffn_w_out[i, 1], ln_g[i, 2], ln_b[i, 2])
    return x
```

```python
import functools

import jax
import jax.numpy as jnp
from jax import lax
from jax.experimental import pallas as pl
from jax.experimental.pallas import tpu as pltpu

GRID_W = 64
N_MIXERS = 2
HG_HEADS = 8
CM_CHUNK = 128
CM_GROUPS = 8
LN_EPS = 1e-5
RMS_EPS = 1e-6
N_MOD = 9

V7X_VMEM_BYTES = 64 * 1024 * 1024
VMEM_LIMIT_BYTES = V7X_VMEM_BYTES - 8 * 1024 * 1024
MXU_DIM = 256
SCAN_CHUNK = 128

F32 = jnp.float32
BF16 = jnp.bfloat16


def _params(*semantics):
    return pltpu.CompilerParams(dimension_semantics=semantics, vmem_limit_bytes=VMEM_LIMIT_BYTES)


def _resident(shape, index_map):
    return pl.BlockSpec(shape, index_map, pipeline_mode=pl.Buffered(1))


def _layer_norm(z, g, b):
    mu = jnp.mean(z, axis=-1, keepdims=True)
    zc = z - mu
    var = jnp.mean(zc * zc, axis=-1, keepdims=True)
    return zc * lax.rsqrt(var + LN_EPS) * g + b


def _mod_rows(mod_ref, k):
    return mod_ref[0, 3 * k:3 * k + 1, :], mod_ref[0, 3 * k + 1:3 * k + 2, :], mod_ref[0, 3 * k + 2:3 * k + 3, :]


def _dot(a, b):
    return jnp.dot(a, b, preferred_element_type=F32)


def _dot_nt(a, b):
    return lax.dot_general(a, b, (((1,), (1,)), ((), ())), preferred_element_type=F32)


def _dot_tn(a, b):
    return lax.dot_general(a, b, (((0,), (0,)), ((), ())), preferred_element_type=F32)


def _modulation_body(cond_ref, w_ref, b_ref, o_ref):
    cond = cond_ref[...]
    act = (cond * jax.nn.sigmoid(cond)).astype(BF16)
    o_ref[0] = _dot(act, w_ref[0].astype(BF16)) + b_ref[0]


def _modulation(cond, mod_w, mod_b):
    depth, d, n = mod_w.shape
    r = cond.shape[0]
    tn = d
    return pl.pallas_call(
        _modulation_body,
        grid=(depth, n // tn),
        in_specs=[
            pl.BlockSpec((r, d), lambda i, j: (0, 0)),
            pl.BlockSpec((1, d, tn), lambda i, j: (i, 0, j)),
            pl.BlockSpec((1, 1, tn), lambda i, j: (i, 0, j)),
        ],
        out_specs=pl.BlockSpec((1, r, tn), lambda i, j: (i, 0, j)),
        out_shape=jax.ShapeDtypeStruct((depth, r, n), F32),
        compiler_params=_params("arbitrary", "arbitrary"),
        name="modulation",
    )(cond, mod_w, mod_b.reshape(depth, 1, n))


def _hidden_chunks(hidden):
    step = 4 * MXU_DIM
    return [(c0, min(step, hidden - c0)) for c0 in range(0, hidden, step)]


def _ffn_body(x_ref, mod_ref, w_in_ref, w_out_ref, g_ref, b_ref, o_ref, *, k, alpha):
    hidden = w_out_ref.shape[0]
    x = x_ref[0]
    shift, scale, gate = _mod_rows(mod_ref, k)
    h = (x * (1.0 + scale) + shift).astype(BF16)
    acc = None
    for c0, cw in _hidden_chunks(hidden):
        gt = _dot(h, w_in_ref[:, c0:c0 + cw])
        up = _dot(h, w_in_ref[:, hidden + c0:hidden + c0 + cw])
        a = (gt * jax.nn.sigmoid(gt) * up).astype(BF16)
        part = _dot(a, w_out_ref[c0:c0 + cw, :])
        acc = part if acc is None else acc + part
    z = alpha * x + gate * (0.5 * acc)
    o_ref[0] = _layer_norm(z, g_ref[...], b_ref[...])


def _ffn(x, mod, k, w_in, w_out, g, b, *, alpha, tm=512):
    bsz, length, d = x.shape
    hidden = w_out.shape[0]
    per_batch = mod.shape[0] > 1
    mod_idx = (lambda i, j: (i, 0, 0)) if per_batch else (lambda i, j: (0, 0, 0))
    return pl.pallas_call(
        functools.partial(_ffn_body, k=k, alpha=alpha),
        grid=(bsz, length // tm),
        in_specs=[
            pl.BlockSpec((1, tm, d), lambda i, j: (i, j, 0)),
            pl.BlockSpec((1, N_MOD, d), mod_idx),
            _resident((d, 2 * hidden), lambda i, j: (0, 0)),
            _resident((hidden, d), lambda i, j: (0, 0)),
            _resident((1, d), lambda i, j: (0, 0)),
            _resident((1, d), lambda i, j: (0, 0)),
        ],
        out_specs=pl.BlockSpec((1, tm, d), lambda i, j: (i, j, 0)),
        out_shape=jax.ShapeDtypeStruct(x.shape, x.dtype),
        compiler_params=_params("arbitrary", "arbitrary"),
        name="swiglu_half_step",
    )(x, mod, w_in, w_out, g.reshape(1, d), b.reshape(1, d))


def _split3(v):
    hi = v.astype(BF16)
    r1 = v - hi.astype(F32)
    mid = r1.astype(BF16)
    lo = (r1 - mid.astype(F32)).astype(BF16)
    return jnp.concatenate([hi, mid, lo], axis=1)


def _scan_chunk(q, k, v, lf, st, keep, mid, last):
    dk = lf.shape[1]
    b3 = _dot(jnp.where(keep, 1.0, 0.0).astype(BF16), _split3(lf))
    b = b3[:, :dk] + b3[:, dk:2 * dk] + b3[:, 2 * dk:]
    r = b[mid:mid + 1, :]
    b_last = b[last:last + 1, :]
    q_in = (q * jnp.exp(b - r)).astype(BF16)
    k_in = (k * jnp.exp(r - b)).astype(BF16)
    v16 = v.astype(BF16)
    scores = jnp.where(keep, _dot_nt(q_in, k_in), 0.0)
    st_ref = (st * jnp.exp(r)).astype(BF16)
    o = _dot(scores.astype(BF16), v16) + _dot_nt(q_in, st_ref)
    kv_t = _dot_tn(v16, k_in)
    st_new = jnp.exp(b_last) * st + jnp.exp(b_last - r) * kv_t
    return o, st_new


def _hgrn_scan_body(x_ref, ctx_ref, mx_ref, mc_ref, w_ref, lbraw_ref, nw_ref, ox_ref, oc_ref,
                    h_scr, q_scr, v_scr, g_scr, kf_scr, lff_scr, kb_scr, lfb_scr, of_scr, ob_scr,
                    *, layer, proj_rows):
    lc, lx = ctx_ref.shape[1], x_ref.shape[1]
    total = lc + lx
    dk = q_scr.shape[1]
    c = SCAN_CHUNK
    n_ctx, n_all = lc // c, total // c

    @pl.when(pl.program_id(1) == 0)
    def _():
        shift, scale, _g = _mod_rows(mc_ref, 1)
        h_scr[0:lc, :] = (ctx_ref[0] * (1.0 + scale) + shift).astype(BF16)
        shift, scale, _g = _mod_rows(mx_ref, 1)
        h_scr[lc:total, :] = (x_ref[0] * (1.0 + scale) + shift).astype(BF16)

    def lower_bound(direction):
        raw = lbraw_ref[0, direction]
        e = jnp.exp(raw - jnp.max(raw, axis=0, keepdims=True))
        return jnp.sum(e[:layer + 1, :], axis=0, keepdims=True) / jnp.sum(e, axis=0, keepdims=True)

    lb_f, lb_b = lower_bound(0), lower_bound(1)

    def project(i, carry):
        rows = pl.ds(pl.multiple_of(i * proj_rows, proj_rows), proj_rows)
        p = _dot(h_scr[rows, :], w_ref[0])
        pq = p[:, 0:dk]
        q_scr[rows, :] = pq * jax.nn.sigmoid(pq)
        v_scr[rows, :] = p[:, dk:2 * dk]
        f_f = lb_f + (1.0 - lb_f) * jax.nn.sigmoid(p[:, 2 * dk:3 * dk])
        kf_scr[rows, :] = 1.0 - f_f
        lff_scr[rows, :] = jnp.log(f_f)
        f_b = lb_b + (1.0 - lb_b) * jax.nn.sigmoid(p[:, 3 * dk:4 * dk])
        kb_scr[rows, :] = 1.0 - f_b
        lfb_scr[rows, :] = jnp.log(f_b)
        g_scr[rows, :] = p[:, 4 * dk:5 * dk]
        return carry

    lax.fori_loop(0, total // proj_rows, project, 0)

    row = lax.broadcasted_iota(jnp.int32, (c, c), 0)
    col = lax.broadcasted_iota(jnp.int32, (c, c), 1)
    keep_f = col <= row
    keep_b = col >= row

    def step(i, carry):
        st_f, st_b = carry
        rows_f = pl.ds(pl.multiple_of(i * c, c), c)
        cb = jnp.where(i < n_ctx, n_ctx - 1 - i, n_all - 1 - (i - n_ctx))
        rows_b = pl.ds(pl.multiple_of(cb * c, c), c)
        o_f, st_f = _scan_chunk(q_scr[rows_f, :], kf_scr[rows_f, :], v_scr[rows_f, :], lff_scr[rows_f, :],
                                st_f, keep_f, c // 2 - 1, c - 1)
        of_scr[rows_f, :] = o_f
        o_b, st_b = _scan_chunk(q_scr[rows_b, :], kb_scr[rows_b, :], v_scr[rows_b, :], lfb_scr[rows_b, :],
                                st_b, keep_b, c // 2, 0)
        ob_scr[rows_b, :] = o_b
        return st_f, st_b

    zero = jnp.zeros((dk, dk), F32)
    lax.fori_loop(0, n_all, step, (zero, zero))

    o = of_scr[...] + ob_scr[...]
    o = o * lax.rsqrt(jnp.mean(o * o, axis=-1, keepdims=True) + RMS_EPS) * nw_ref[...]
    g = g_scr[...]
    o = (o * (g * jax.nn.sigmoid(g))).astype(BF16)
    oc_ref[0] = o[0:lc, :]
    ox_ref[0] = o[lc:total, :]


def _hgrn_scan(x, ctx, mx, mc, w_heads, lb_raw, norm_w, *, layer):
    bsz, lx, d = x.shape
    lc = ctx.shape[1]
    nh, _, five_dk = w_heads.shape
    dk = five_dk // 5
    total = lc + lx
    seq = lambda: pltpu.VMEM((total, dk), F32)
    return pl.pallas_call(
        functools.partial(_hgrn_scan_body, layer=layer, proj_rows=256),
        grid=(bsz, nh),
        in_specs=[
            pl.BlockSpec((1, lx, d), lambda b, h: (b, 0, 0)),
            pl.BlockSpec((1, lc, d), lambda b, h: (b, 0, 0)),
            pl.BlockSpec((1, N_MOD, d), lambda b, h: (b, 0, 0)),
            _resident((1, N_MOD, d), lambda b, h: (0, 0, 0)),
            pl.BlockSpec((1, d, five_dk), lambda b, h: (h, 0, 0)),
            pl.BlockSpec((1,) + lb_raw.shape[1:], lambda b, h: (h, 0, 0, 0)),
            _resident((1, dk), lambda b, h: (0, 0)),
        ],
        out_specs=[
            pl.BlockSpec((1, lx, dk), lambda b, h: (b, 0, h)),
            pl.BlockSpec((1, lc, dk), lambda b, h: (b, 0, h)),
        ],
        out_shape=[
            jax.ShapeDtypeStruct((bsz, lx, d), BF16),
            jax.ShapeDtypeStruct((bsz, lc, d), BF16),
        ],
        scratch_shapes=[pltpu.VMEM((total, d), BF16)] + [seq() for _ in range(9)],
        compiler_params=_params("arbitrary", "arbitrary"),
        name="hgrn2_scan",
    )(x, ctx, mx, mc, w_heads, lb_raw, norm_w.reshape(1, dk))


def _proj_postnorm_body(x_ref, o_ref, mod_ref, w_ref, g_ref, b_ref, out_ref, *, k, alpha):
    x = x_ref[0]
    _s, _c, gate = _mod_rows(mod_ref, k)
    y = _dot(o_ref[0], w_ref[...])
    out_ref[0] = _layer_norm(alpha * x + gate * y, g_ref[...], b_ref[...])


def _proj_postnorm(x, o, mod, k, w, g, b, *, alpha, tm=512):
    bsz, length, d = x.shape
    kdim = o.shape[2]
    per_batch = mod.shape[0] > 1
    mod_idx = (lambda i, j: (i, 0, 0)) if per_batch else (lambda i, j: (0, 0, 0))
    return pl.pallas_call(
        functools.partial(_proj_postnorm_body, k=k, alpha=alpha),
        grid=(bsz, length // tm),
        in_specs=[
            pl.BlockSpec((1, tm, d), lambda i, j: (i, j, 0)),
            pl.BlockSpec((1, tm, kdim), lambda i, j: (i, j, 0)),
            pl.BlockSpec((1, N_MOD, d), mod_idx),
            _resident((kdim, d), lambda i, j: (0, 0)),
            _resident((1, d), lambda i, j: (0, 0)),
            _resident((1, d), lambda i, j: (0, 0)),
        ],
        out_specs=pl.BlockSpec((1, tm, d), lambda i, j: (i, j, 0)),
        out_shape=jax.ShapeDtypeStruct(x.shape, x.dtype),
        compiler_params=_params("arbitrary", "arbitrary"),
        name="proj_postnorm",
    )(x, o, mod, w, g.reshape(1, d), b.reshape(1, d))


def _chunk_mlp_body(x_ref, mod_ref, w_in_ref, vg_ref, vb_ref, ws_ref, bs_ref, w_out_ref, g_ref, b_ref, o_ref,
                    *, k, alpha):
    inner = w_out_ref.shape[0]
    groups, chunk, _ = ws_ref.shape
    gw = inner // groups
    tm = x_ref.shape[1]
    x = x_ref[0]
    shift, scale, gate = _mod_rows(mod_ref, k)
    h = (x * (1.0 + scale) + shift).astype(BF16)
    u = jax.nn.gelu(_dot(h, w_in_ref[:, 0:inner]), approximate=True)
    v = jax.nn.gelu(_dot(h, w_in_ref[:, inner:2 * inner]), approximate=True)
    v = _layer_norm(v, vg_ref[...], vb_ref[...]).astype(BF16)
    pieces = []
    for ci in range(tm // chunk):
        r0 = ci * chunk
        cols = []
        for gi in range(groups):
            sv = _dot(ws_ref[gi], v[r0:r0 + chunk, gi * gw:(gi + 1) * gw]) + bs_ref[gi]
            cols.append((u[r0:r0 + chunk, gi * gw:(gi + 1) * gw] * sv).astype(BF16))
        pieces.append(jnp.concatenate(cols, axis=1))
    gated = jnp.concatenate(pieces, axis=0)
    y = _dot(gated, w_out_ref[...])
    o_ref[0] = _layer_norm(alpha * x + gate * y, g_ref[...], b_ref[...])


def _chunk_mlp(x, mod, k, w_in, v_g, v_b, w_s, b_s, w_out, g, b, *, alpha, tm=256):
    bsz, length, d = x.shape
    inner = w_out.shape[0]
    groups, chunk, _ = w_s.shape
    per_batch = mod.shape[0] > 1
    mod_idx = (lambda i, j: (i, 0, 0)) if per_batch else (lambda i, j: (0, 0, 0))
    return pl.pallas_call(
        functools.partial(_chunk_mlp_body, k=k, alpha=alpha),
        grid=(bsz, length // tm),
        in_specs=[
            pl.BlockSpec((1, tm, d), lambda i, j: (i, j, 0)),
            pl.BlockSpec((1, N_MOD, d), mod_idx),
            _resident((d, 2 * inner), lambda i, j: (0, 0)),
            _resident((1, inner), lambda i, j: (0, 0)),
            _resident((1, inner), lambda i, j: (0, 0)),
            _resident((groups, chunk, chunk), lambda i, j: (0, 0, 0)),
            _resident((groups, chunk, 1), lambda i, j: (0, 0, 0)),
            _resident((inner, d), lambda i, j: (0, 0)),
            _resident((1, d), lambda i, j: (0, 0)),
            _resident((1, d), lambda i, j: (0, 0)),
        ],
        out_specs=pl.BlockSpec((1, tm, d), lambda i, j: (i, j, 0)),
        out_shape=jax.ShapeDtypeStruct(x.shape, x.dtype),
        compiler_params=_params("arbitrary", "arbitrary"),
        name="chunk_mlp",
    )(x, mod, w_in, v_g.reshape(1, inner), v_b.reshape(1, inner), w_s, b_s.reshape(groups, chunk, 1), w_out,
      g.reshape(1, d), b.reshape(1, d))


def kernel(x, c, ctx, c_ctx, mod_w, mod_b, ln_g, ln_b, ffn_w_in, ffn_w_out, hg_w_in, hg_lower_bounds, hg_norm_w,
           hg_w_out, cm_w_in, cm_v_g, cm_v_b, cm_w_s, cm_b_s, cm_w_out):
    depth = mod_w.shape[0]
    bsz, _, d = x.shape
    dk = d // HG_HEADS
    alpha = (2 * depth) ** 0.25
    assert CM_CHUNK % GRID_W == 0 and x.shape[1] % CM_CHUNK == 0 and ctx.shape[1] % CM_CHUNK == 0

    pad = (-(bsz + 1)) % 8
    cond = jnp.concatenate([c, c_ctx[None, :], jnp.zeros((pad, d), c.dtype)], axis=0)
    m_all = _modulation(cond, mod_w, mod_b)

    ffn_w_in16, ffn_w_out16 = ffn_w_in.astype(BF16), ffn_w_out.astype(BF16)
    hg_w_heads = hg_w_in.reshape(-1, d, 5, HG_HEADS, dk).transpose(0, 3, 1, 2, 4).reshape(-1, HG_HEADS, d, 5 * dk)
    hg_w_heads = hg_w_heads.astype(BF16)
    hg_w_out16 = hg_w_out.astype(BF16)
    lb_raw = hg_lower_bounds.astype(F32).reshape(2, depth + 1, HG_HEADS, dk).transpose(2, 0, 1, 3)
    cm_w_in16, cm_w_out16, cm_w_s16 = cm_w_in.astype(BF16), cm_w_out.astype(BF16), cm_w_s.astype(BF16)

    ctx_shape = ctx.shape
    flat = lambda t: t.reshape(1, -1, d)
    for i in range(depth):
        last = i == depth - 1
        kind = i % N_MIXERS
        j = i // N_MIXERS
        ctx_needed = (not last) or kind == 0
        mx = m_all[i, :bsz].reshape(bsz, N_MOD, d)
        mc = m_all[i, bsz].reshape(1, N_MOD, d)
        ffn = functools.partial(_ffn, alpha=alpha)
        x = ffn(x, mx, 0, ffn_w_in16[i, 0], ffn_w_out16[i, 0], ln_g[i, 0], ln_b[i, 0])
        if ctx_needed:
            ctx = ffn(flat(ctx), mc, 0, ffn_w_in16[i, 0], ffn_w_out16[i, 0], ln_g[i, 0], ln_b[i, 0]).reshape(ctx_shape)
        if kind == 0:
            ox, oc = _hgrn_scan(x, ctx, mx, mc, hg_w_heads[j], lb_raw, hg_norm_w[j], layer=i)
            x = _proj_postnorm(x, ox, mx, 1, hg_w_out16[j], ln_g[i, 1], ln_b[i, 1], alpha=alpha)
            if not last:
                ctx = _proj_postnorm(flat(ctx), flat(oc), mc, 1, hg_w_out16[j], ln_g[i, 1], ln_b[i, 1],
                                     alpha=alpha).reshape(ctx_shape)
        else:
            cm = functools.partial(_chunk_mlp, alpha=alpha)
            x_in = x
            x = cm(x_in, mx, 1, cm_w_in16[j], cm_v_g[j], cm_v_b[j], cm_w_s16[j], cm_b_s[j], cm_w_out16[j],
                   ln_g[i, 1], ln_b[i, 1])
            if not last:
                ctx = cm(ctx, mc, 1, cm_w_in16[j], cm_v_g[j], cm_v_b[j], cm_w_s16[j], cm_b_s[j], cm_w_out16[j],
                         ln_g[i, 1], ln_b[i, 1])
        x = ffn(x, mx, 2, ffn_w_in16[i, 1], ffn_w_out16[i, 1], ln_g[i, 2], ln_b[i, 2])
        if not last:
            ctx = ffn(flat(ctx), mc, 2, ffn_w_in16[i, 1], ffn_w_out16[i, 1], ln_g[i, 2], ln_b[i, 2]).reshape(ctx_shape)
    return x
```

```python
import functools

import jax
import jax.numpy as jnp
from jax import lax
from jax.experimental import pallas as pl
from jax.experimental.pallas import tpu as pltpu

GRID_W = 64
N_MIXERS = 2
HG_HEADS = 8
CM_CHUNK = 128
CM_GROUPS = 8
LN_EPS = 1e-5
RMS_EPS = 1e-6
N_MOD = 9

V7X_VMEM_BYTES = 64 * 1024 * 1024
VMEM_LIMIT_BYTES = V7X_VMEM_BYTES - 8 * 1024 * 1024
MXU_DIM = 256
SCAN_CHUNK = 128

F32 = jnp.float32
BF16 = jnp.bfloat16


def _params(*semantics):
    return pltpu.CompilerParams(dimension_semantics=semantics, vmem_limit_bytes=VMEM_LIMIT_BYTES)


def _resident(shape, index_map):
    return pl.BlockSpec(shape, index_map, pipeline_mode=pl.Buffered(1))


def _layer_norm(z, g, b):
    mu = jnp.mean(z, axis=-1, keepdims=True)
    zc = z - mu
    var = jnp.mean(zc * zc, axis=-1, keepdims=True)
    return zc * lax.rsqrt(var + LN_EPS) * g + b


def _mod_rows(mod_ref, k):
    return mod_ref[0, 3 * k:3 * k + 1, :], mod_ref[0, 3 * k + 1:3 * k + 2, :], mod_ref[0, 3 * k + 2:3 * k + 3, :]


def _mod_index(mod):
    return (lambda i, j: (i, 0, 0)) if mod.shape[0] > 1 else (lambda i, j: (0, 0, 0))


def _dot(a, b):
    return jnp.dot(a, b, preferred_element_type=F32)


def _dot_nt(a, b):
    return lax.dot_general(a, b, (((1,), (1,)), ((), ())), preferred_element_type=F32)


def _dot_tn(a, b):
    return lax.dot_general(a, b, (((0,), (0,)), ((), ())), preferred_element_type=F32)


def _modulation_body(cond_ref, w_ref, b_ref, o_ref):
    cond = cond_ref[...]
    act = (cond * jax.nn.sigmoid(cond)).astype(BF16)
    o_ref[0] = _dot(act, w_ref[0].astype(BF16)) + b_ref[0]


def _modulation(cond, mod_w, mod_b):
    depth, d, n = mod_w.shape
    r = cond.shape[0]
    tn = d
    return pl.pallas_call(
        _modulation_body,
        grid=(depth, n // tn),
        in_specs=[
            pl.BlockSpec((r, d), lambda i, j: (0, 0)),
            pl.BlockSpec((1, d, tn), lambda i, j: (i, 0, j)),
            pl.BlockSpec((1, 1, tn), lambda i, j: (i, 0, j)),
        ],
        out_specs=pl.BlockSpec((1, r, tn), lambda i, j: (i, 0, j)),
        out_shape=jax.ShapeDtypeStruct((depth, r, n), F32),
        compiler_params=_params("arbitrary", "arbitrary"),
        name="modulation",
    )(cond, mod_w, mod_b.reshape(depth, 1, n))


def _hidden_chunks(hidden):
    step = 4 * MXU_DIM
    return [(c0, min(step, hidden - c0)) for c0 in range(0, hidden, step)]


def _ffn_body(x_ref, mod_ref, w_in_ref, w_out_ref, g_ref, b_ref, o_ref, *, k, alpha):
    hidden = w_out_ref.shape[0]
    x = x_ref[0]
    shift, scale, gate = _mod_rows(mod_ref, k)
    h = (x * (1.0 + scale) + shift).astype(BF16)
    acc = None
    for c0, cw in _hidden_chunks(hidden):
        gt = _dot(h, w_in_ref[:, c0:c0 + cw])
        up = _dot(h, w_in_ref[:, hidden + c0:hidden + c0 + cw])
        a = (gt * jax.nn.sigmoid(gt) * up).astype(BF16)
        part = _dot(a, w_out_ref[c0:c0 + cw, :])
        acc = part if acc is None else acc + part
    z = alpha * x + gate * (0.5 * acc)
    o_ref[0] = _layer_norm(z, g_ref[...], b_ref[...])


def _ffn(x, mod, k, w_in, w_out, g, b, *, alpha, tm=512):
    bsz, length, d = x.shape
    hidden = w_out.shape[0]
    return pl.pallas_call(
        functools.partial(_ffn_body, k=k, alpha=alpha),
        grid=(bsz, length // tm),
        in_specs=[
            pl.BlockSpec((1, tm, d), lambda i, j: (i, j, 0)),
            pl.BlockSpec((1, N_MOD, d), _mod_index(mod)),
            _resident((d, 2 * hidden), lambda i, j: (0, 0)),
            _resident((hidden, d), lambda i, j: (0, 0)),
            _resident((1, d), lambda i, j: (0, 0)),
            _resident((1, d), lambda i, j: (0, 0)),
        ],
        out_specs=pl.BlockSpec((1, tm, d), lambda i, j: (i, j, 0)),
        out_shape=jax.ShapeDtypeStruct(x.shape, x.dtype),
        compiler_params=_params("arbitrary", "arbitrary"),
        name="swiglu_half_step",
    )(x, mod, w_in, w_out, g.reshape(1, d), b.reshape(1, d))


HG_STREAMS = (BF16, BF16, BF16, BF16, BF16, F32, F32)


def _hgrn_proj_body(x_ref, mod_ref, w_ref, lbraw_ref, q_ref, v_ref, g_ref, kf_ref, kb_ref, lff_ref, lfb_ref,
                    *, layer):
    d = x_ref.shape[2]
    shift, scale, _gate = _mod_rows(mod_ref, 1)
    h = (x_ref[0] * (1.0 + scale) + shift).astype(BF16)

    def proj(s):
        return _dot(h, w_ref[:, s * d:(s + 1) * d])

    pq = proj(0)
    q_ref[0] = (pq * jax.nn.sigmoid(pq)).astype(BF16)
    v_ref[0] = proj(1).astype(BF16)
    for direction, k_ref, lf_ref in ((0, kf_ref, lff_ref), (1, kb_ref, lfb_ref)):
        raw = lbraw_ref[direction]
        e = jnp.exp(raw - jnp.max(raw, axis=0, keepdims=True))
        lb = jnp.sum(e[:layer + 1, :], axis=0, keepdims=True) / jnp.sum(e, axis=0, keepdims=True)
        f = lb + (1.0 - lb) * jax.nn.sigmoid(proj(2 + direction))
        k_ref[0] = (1.0 - f).astype(BF16)
        lf_ref[0] = jnp.log(f)
    g_ref[0] = proj(4).astype(BF16)


def _hgrn_project(x, mod, w_in, lb_raw, *, layer, tm=512):
    bsz, length, d = x.shape
    tile = pl.BlockSpec((1, tm, d), lambda i, j: (i, j, 0))
    return pl.pallas_call(
        functools.partial(_hgrn_proj_body, layer=layer),
        grid=(bsz, length // tm),
        in_specs=[
            tile,
            pl.BlockSpec((1, N_MOD, d), _mod_index(mod)),
            _resident(w_in.shape, lambda i, j: (0, 0)),
            _resident(lb_raw.shape, lambda i, j: (0, 0, 0)),
        ],
        out_specs=[tile] * len(HG_STREAMS),
        out_shape=[jax.ShapeDtypeStruct(x.shape, dt) for dt in HG_STREAMS],
        compiler_params=_params("arbitrary", "arbitrary"),
        name="hgrn2_project",
    )(x, mod, w_in, lb_raw)


def _split2(v):
    hi = v.astype(BF16)
    lo = (v - hi.astype(F32)).astype(BF16)
    return hi, lo


def _hgrn_scan_body(*refs):
    ns = len(HG_STREAMS)
    ctx_in, x_in = refs[0:ns], refs[ns:2 * ns]
    nw_ref, ox_ref, oc_ref = refs[2 * ns:2 * ns + 3]
    v_scr, m_scr, qin_scr, kin_scr, kvt_scr, ste_scr, vec_scr = refs[2 * ns + 3:]
    lc, lx = ctx_in[0].shape[1], x_in[0].shape[1]
    dk = x_in[0].shape[2]
    c = SCAN_CHUNK
    n_ctx, n_all = lc // c, (lc + lx) // c
    mid_f, mid_b = c // 2 - 1, c // 2

    row = lax.broadcasted_iota(jnp.int32, (c, c), 0)
    col = lax.broadcasted_iota(jnp.int32, (c, c), 1)
    keep_f = col <= row
    keep_b = col >= row
    tri = jnp.where(keep_f, 1.0, 0.0).astype(BF16)

    def chunk_rows(n):
        return pl.ds(pl.multiple_of(n * c, c), c)

    def for_segments(fn, unroll):
        for ins, out_ref, first, count in ((ctx_in, oc_ref, 0, n_ctx), (x_in, ox_ref, n_ctx, n_all - n_ctx)):
            def body(j, carry, ins=ins, out_ref=out_ref, first=first):
                fn(ins, out_ref, j, first + j)
                return carry
            lax.fori_loop(0, count, body, 0, unroll=min(unroll, count))

    def decays(ins, _out, j, n):
        q_ref, v_ref, _g, kf_ref, kb_ref, lff_ref, lfb_ref = ins
        src, rows = chunk_rows(j), chunk_rows(n)
        lf_f, lf_b = lff_ref[0, src, :], lfb_ref[0, src, :]
        prefix = _dot(tri, jnp.concatenate(_split2(lf_f) + _split2(lf_b), axis=1))
        b_f = prefix[:, 0:dk] + prefix[:, dk:2 * dk]
        p_b = prefix[:, 2 * dk:3 * dk] + prefix[:, 3 * dk:4 * dk]
        b_b = p_b[c - 1:c, :] - p_b + lf_b
        q = q_ref[0, src, :].astype(F32)
        v_scr[rows, :] = v_ref[0, src, :]
        for lo, b, r_row, end_row, k_ref in ((0, b_f, mid_f, c - 1, kf_ref), (dk, b_b, mid_b, 0, kb_ref)):
            r = b[r_row:r_row + 1, :]
            b_end = b[end_row:end_row + 1, :]
            qin_scr[rows, lo:lo + dk] = (q * jnp.exp(b - r)).astype(BF16)
            kin_scr[rows, lo:lo + dk] = (k_ref[0, src, :].astype(F32) * jnp.exp(r - b)).astype(BF16)
            vec_scr[n, :, lo:lo + dk] = jnp.concatenate(
                [jnp.exp(r), jnp.exp(b_end - r), jnp.exp(b_end), jnp.zeros((5, dk), F32)], axis=0)

    for_segments(decays, 4)

    def intra(n, carry):
        rows = chunk_rows(n)
        s_f = _dot_nt(qin_scr[rows, 0:dk], kin_scr[rows, 0:dk])
        s_b = _dot_nt(qin_scr[rows, dk:2 * dk], kin_scr[rows, dk:2 * dk])
        m_scr[rows, :] = (jnp.where(keep_f, s_f, 0.0) + jnp.where(keep_b, s_b, 0.0)).astype(BF16)
        kvt_scr[n] = _dot_tn(v_scr[rows, :], kin_scr[rows, :])
        return carry

    lax.fori_loop(0, n_all, intra, 0, unroll=6)

    def carry_state(i, carry):
        st_f, st_b = carry
        cb = jnp.where(i < n_ctx, n_ctx - 1 - i, n_all - 1 - (i - n_ctx))
        vec_f, vec_b = vec_scr[i, :, 0:dk], vec_scr[cb, :, dk:2 * dk]
        ste_scr[i, :, 0:dk] = (st_f * vec_f[0:1, :]).astype(BF16)
        ste_scr[cb, :, dk:2 * dk] = (st_b * vec_b[0:1, :]).astype(BF16)
        st_f = vec_f[2:3, :] * st_f + vec_f[1:2, :] * kvt_scr[i, :, 0:dk]
        st_b = vec_b[2:3, :] * st_b + vec_b[1:2, :] * kvt_scr[cb, :, dk:2 * dk]
        return st_f, st_b

    zero = jnp.zeros((dk, dk), F32)
    lax.fori_loop(0, n_all, carry_state, (zero, zero), unroll=6)

    def readout(ins, out_ref, j, n):
        src, rows = chunk_rows(j), chunk_rows(n)
        o = _dot(m_scr[rows, :], v_scr[rows, :]) + _dot_nt(qin_scr[rows, :], ste_scr[n])
        o = o * lax.rsqrt(jnp.mean(o * o, axis=-1, keepdims=True) + RMS_EPS) * nw_ref[...]
        g = ins[2][0, src, :].astype(F32)
        out_ref[0, src, :] = (o * (g * jax.nn.sigmoid(g))).astype(BF16)

    for_segments(readout, 4)


def _hgrn_scan(ctx_streams, x_streams, norm_w):
    bsz, lx, d = x_streams[0].shape
    lc = ctx_streams[0].shape[1]
    dk = d // HG_HEADS
    total = lc + lx
    n_all = total // SCAN_CHUNK
    head = lambda length: pl.BlockSpec((1, length, dk), lambda b, h: (b, 0, h))
    seq = lambda width: pltpu.VMEM((total, width), BF16)
    return pl.pallas_call(
        _hgrn_scan_body,
        grid=(bsz, HG_HEADS),
        in_specs=[head(lc)] * len(ctx_streams) + [head(lx)] * len(x_streams) + [_resident((1, dk), lambda b, h: (0, 0))],
        out_specs=[head(lx), head(lc)],
        out_shape=[jax.ShapeDtypeStruct((bsz, lx, d), BF16), jax.ShapeDtypeStruct((bsz, lc, d), BF16)],
        scratch_shapes=[
            seq(dk), seq(dk),
            seq(2 * dk), seq(2 * dk),
            pltpu.VMEM((n_all, dk, 2 * dk), F32),
            pltpu.VMEM((n_all, dk, 2 * dk), BF16),
            pltpu.VMEM((n_all, 8, 2 * dk), F32),
        ],
        compiler_params=_params("arbitrary", "arbitrary"),
        name="hgrn2_scan",
    )(*ctx_streams, *x_streams, norm_w.reshape(1, dk))


def _proj_postnorm_body(x_ref, o_ref, mod_ref, w_ref, g_ref, b_ref, out_ref, *, k, alpha):
    x = x_ref[0]
    _s, _c, gate = _mod_rows(mod_ref, k)
    y = _dot(o_ref[0], w_ref[...])
    out_ref[0] = _layer_norm(alpha * x + gate * y, g_ref[...], b_ref[...])


def _proj_postnorm(x, o, mod, k, w, g, b, *, alpha, tm=512):
    bsz, length, d = x.shape
    kdim = o.shape[2]
    return pl.pallas_call(
        functools.partial(_proj_postnorm_body, k=k, alpha=alpha),
        grid=(bsz, length // tm),
        in_specs=[
            pl.BlockSpec((1, tm, d), lambda i, j: (i, j, 0)),
            pl.BlockSpec((1, tm, kdim), lambda i, j: (i, j, 0)),
            pl.BlockSpec((1, N_MOD, d), _mod_index(mod)),
            _resident((kdim, d), lambda i, j: (0, 0)),
            _resident((1, d), lambda i, j: (0, 0)),
            _resident((1, d), lambda i, j: (0, 0)),
        ],
        out_specs=pl.BlockSpec((1, tm, d), lambda i, j: (i, j, 0)),
        out_shape=jax.ShapeDtypeStruct(x.shape, x.dtype),
        compiler_params=_params("arbitrary", "arbitrary"),
        name="proj_postnorm",
    )(x, o, mod, w, g.reshape(1, d), b.reshape(1, d))


def _chunk_mlp_body(x_ref, mod_ref, w_in_ref, vg_ref, vb_ref, ws_ref, bs_ref, w_out_ref, g_ref, b_ref, o_ref,
                    *, k, alpha):
    inner = w_out_ref.shape[0]
    groups, chunk, _ = ws_ref.shape
    gw = inner // groups
    tm = x_ref.shape[1]
    x = x_ref[0]
    shift, scale, gate = _mod_rows(mod_ref, k)
    h = (x * (1.0 + scale) + shift).astype(BF16)
    u = jax.nn.gelu(_dot(h, w_in_ref[:, 0:inner]), approximate=True)
    v = jax.nn.gelu(_dot(h, w_in_ref[:, inner:2 * inner]), approximate=True)
    v = _layer_norm(v, vg_ref[...], vb_ref[...]).astype(BF16)
    pieces = []
    for ci in range(tm // chunk):
        r0 = ci * chunk
        cols = []
        for gi in range(groups):
            sv = _dot(ws_ref[gi], v[r0:r0 + chunk, gi * gw:(gi + 1) * gw]) + bs_ref[gi]
            cols.append((u[r0:r0 + chunk, gi * gw:(gi + 1) * gw] * sv).astype(BF16))
        pieces.append(jnp.concatenate(cols, axis=1))
    gated = jnp.concatenate(pieces, axis=0)
    y = _dot(gated, w_out_ref[...])
    o_ref[0] = _layer_norm(alpha * x + gate * y, g_ref[...], b_ref[...])


def _chunk_mlp(x, mod, k, w_in, v_g, v_b, w_s, b_s, w_out, g, b, *, alpha, tm=256):
    bsz, length, d = x.shape
    inner = w_out.shape[0]
    groups, chunk, _ = w_s.shape
    return pl.pallas_call(
        functools.partial(_chunk_mlp_body, k=k, alpha=alpha),
        grid=(bsz, length // tm),
        in_specs=[
            pl.BlockSpec((1, tm, d), lambda i, j: (i, j, 0)),
            pl.BlockSpec((1, N_MOD, d), _mod_index(mod)),
            _resident((d, 2 * inner), lambda i, j: (0, 0)),
            _resident((1, inner), lambda i, j: (0, 0)),
            _resident((1, inner), lambda i, j: (0, 0)),
            _resident((groups, chunk, chunk), lambda i, j: (0, 0, 0)),
            _resident((groups, chunk, 1), lambda i, j: (0, 0, 0)),
            _resident((inner, d), lambda i, j: (0, 0)),
            _resident((1, d), lambda i, j: (0, 0)),
            _resident((1, d), lambda i, j: (0, 0)),
        ],
        out_specs=pl.BlockSpec((1, tm, d), lambda i, j: (i, j, 0)),
        out_shape=jax.ShapeDtypeStruct(x.shape, x.dtype),
        compiler_params=_params("arbitrary", "arbitrary"),
        name="chunk_mlp",
    )(x, mod, w_in, v_g.reshape(1, inner), v_b.reshape(1, inner), w_s, b_s.reshape(groups, chunk, 1), w_out,
      g.reshape(1, d), b.reshape(1, d))


def kernel(x, c, ctx, c_ctx, mod_w, mod_b, ln_g, ln_b, ffn_w_in, ffn_w_out, hg_w_in, hg_lower_bounds, hg_norm_w,
           hg_w_out, cm_w_in, cm_v_g, cm_v_b, cm_w_s, cm_b_s, cm_w_out):
    depth = mod_w.shape[0]
    bsz, _, d = x.shape
    alpha = (2 * depth) ** 0.25
    assert CM_CHUNK % GRID_W == 0 and x.shape[1] % CM_CHUNK == 0 and ctx.shape[1] % CM_CHUNK == 0

    pad = (-(bsz + 1)) % 8
    cond = jnp.concatenate([c, c_ctx[None, :], jnp.zeros((pad, d), c.dtype)], axis=0)
    m_all = _modulation(cond, mod_w, mod_b)

    ffn_w_in16, ffn_w_out16 = ffn_w_in.astype(BF16), ffn_w_out.astype(BF16)
    hg_w_in16, hg_w_out16 = hg_w_in.astype(BF16), hg_w_out.astype(BF16)
    lb_raw = hg_lower_bounds.astype(F32)
    cm_w_in16, cm_w_out16, cm_w_s16 = cm_w_in.astype(BF16), cm_w_out.astype(BF16), cm_w_s.astype(BF16)

    ctx_shape = ctx.shape
    flat = lambda t: t.reshape(1, -1, d)
    for i in range(depth):
        last = i == depth - 1
        kind = i % N_MIXERS
        j = i // N_MIXERS
        ctx_needed = (not last) or kind == 0
        mx = m_all[i, :bsz].reshape(bsz, N_MOD, d)
        mc = m_all[i, bsz].reshape(1, N_MOD, d)
        ffn = functools.partial(_ffn, alpha=alpha)
        x = ffn(x, mx, 0, ffn_w_in16[i, 0], ffn_w_out16[i, 0], ln_g[i, 0], ln_b[i, 0])
        if ctx_needed:
            ctx = ffn(flat(ctx), mc, 0, ffn_w_in16[i, 0], ffn_w_out16[i, 0], ln_g[i, 0], ln_b[i, 0]).reshape(ctx_shape)
        if kind == 0:
            x_streams = _hgrn_project(x, mx, hg_w_in16[j], lb_raw, layer=i)
            ctx_streams = [s.reshape(ctx_shape) for s in _hgrn_project(flat(ctx), mc, hg_w_in16[j], lb_raw, layer=i)]
            ox, oc = _hgrn_scan(ctx_streams, x_streams, hg_norm_w[j])
            x = _proj_postnorm(x, ox, mx, 1, hg_w_out16[j], ln_g[i, 1], ln_b[i, 1], alpha=alpha)
            if not last:
                ctx = _proj_postnorm(flat(ctx), flat(oc), mc, 1, hg_w_out16[j], ln_g[i, 1], ln_b[i, 1],
                                     alpha=alpha).reshape(ctx_shape)
        else:
            cm = functools.partial(_chunk_mlp, alpha=alpha)
            x = cm(x, mx, 1, cm_w_in16[j], cm_v_g[j], cm_v_b[j], cm_w_s16[j], cm_b_s[j], cm_w_out16[j],
                   ln_g[i, 1], ln_b[i, 1])
            if not last:
                ctx = cm(ctx, mc, 1, cm_w_in16[j], cm_v_g[j], cm_v_b[j], cm_w_s16[j], cm_b_s[j], cm_w_out16[j],
                         ln_g[i, 1], ln_b[i, 1])
        x = ffn(x, mx, 2, ffn_w_in16[i, 1], ffn_w_out16[i, 1], ln_g[i, 2], ln_b[i, 2])
        if not last:
            ctx = ffn(flat(ctx), mc, 2, ffn_w_in16[i, 1], ffn_w_out16[i, 1], ln_g[i, 2], ln_b[i, 2]).reshape(ctx_shape)
    return x
```

```python
import functools

import jax
import jax.numpy as jnp
from jax import lax
from jax.experimental import pallas as pl
from jax.experimental.pallas import tpu as pltpu

GRID_W = 64
N_MIXERS = 2
HG_HEADS = 8
CM_CHUNK = 128
CM_GROUPS = 8
LN_EPS = 1e-5
RMS_EPS = 1e-6
N_MOD = 9

V7X_VMEM_BYTES = 64 * 1024 * 1024
VMEM_LIMIT_BYTES = V7X_VMEM_BYTES - 8 * 1024 * 1024
MXU_DIM = 256
SCAN_CHUNK = 128

F32 = jnp.float32
BF16 = jnp.bfloat16


def _params(*semantics):
    return pltpu.CompilerParams(dimension_semantics=semantics, vmem_limit_bytes=VMEM_LIMIT_BYTES)


def _resident(shape, index_map):
    return pl.BlockSpec(shape, index_map, pipeline_mode=pl.Buffered(1))


def _layer_norm(z, g, b):
    mu = jnp.mean(z, axis=-1, keepdims=True)
    zc = z - mu
    var = jnp.mean(zc * zc, axis=-1, keepdims=True)
    return zc * lax.rsqrt(var + LN_EPS) * g + b


def _mod_rows(mod_ref, k):
    return mod_ref[0, 3 * k:3 * k + 1, :], mod_ref[0, 3 * k + 1:3 * k + 2, :], mod_ref[0, 3 * k + 2:3 * k + 3, :]


def _mod_index(mod):
    return (lambda i, j: (i, 0, 0)) if mod.shape[0] > 1 else (lambda i, j: (0, 0, 0))


def _dot(a, b):
    return jnp.dot(a, b, preferred_element_type=F32)


def _dot_nt(a, b):
    return lax.dot_general(a, b, (((1,), (1,)), ((), ())), preferred_element_type=F32)


def _dot_tn(a, b):
    return lax.dot_general(a, b, (((0,), (0,)), ((), ())), preferred_element_type=F32)


def _modulation_body(cond_ref, w_ref, b_ref, o_ref):
    cond = cond_ref[...]
    act = (cond * jax.nn.sigmoid(cond)).astype(BF16)
    o_ref[0] = _dot(act, w_ref[0].astype(BF16)) + b_ref[0]


def _modulation(cond, mod_w, mod_b):
    depth, d, n = mod_w.shape
    r = cond.shape[0]
    tn = d
    return pl.pallas_call(
        _modulation_body,
        grid=(depth, n // tn),
        in_specs=[
            pl.BlockSpec((r, d), lambda i, j: (0, 0)),
            pl.BlockSpec((1, d, tn), lambda i, j: (i, 0, j)),
            pl.BlockSpec((1, 1, tn), lambda i, j: (i, 0, j)),
        ],
        out_specs=pl.BlockSpec((1, r, tn), lambda i, j: (i, 0, j)),
        out_shape=jax.ShapeDtypeStruct((depth, r, n), F32),
        compiler_params=_params("arbitrary", "arbitrary"),
        name="modulation",
    )(cond, mod_w, mod_b.reshape(depth, 1, n))


def _hidden_chunks(hidden):
    step = 4 * MXU_DIM
    return [(c0, min(step, hidden - c0)) for c0 in range(0, hidden, step)]


def _half_step(x, mod_ref, k, w_in_ref, w_out_ref, g_ref, b_ref, alpha):
    hidden = w_out_ref.shape[0]
    shift, scale, gate = _mod_rows(mod_ref, k)
    h = (x * (1.0 + scale) + shift).astype(BF16)
    acc = None
    for c0, cw in _hidden_chunks(hidden):
        gt = _dot(h, w_in_ref[:, c0:c0 + cw])
        up = _dot(h, w_in_ref[:, hidden + c0:hidden + c0 + cw])
        a = (gt * jax.nn.sigmoid(gt) * up).astype(BF16)
        part = _dot(a, w_out_ref[c0:c0 + cw, :])
        acc = part if acc is None else acc + part
    z = alpha * x + gate * (0.5 * acc)
    return _layer_norm(z, g_ref[...], b_ref[...])


def _ffn_body(x_ref, mod_ref, w_in_ref, w_out_ref, g_ref, b_ref, o_ref, *, k, alpha):
    o_ref[0] = _half_step(x_ref[0], mod_ref, k, w_in_ref, w_out_ref, g_ref, b_ref, alpha)


def _proj_ffn_body(x_ref, y_ref, mod_ref, wp_ref, gp_ref, bp_ref, w_in_ref, w_out_ref, g_ref, b_ref, o_ref,
                   *, k, alpha):
    _s, _c, gate = _mod_rows(mod_ref, k)
    x = _layer_norm(alpha * x_ref[0] + gate * _dot(y_ref[0], wp_ref[...]), gp_ref[...], bp_ref[...])
    o_ref[0] = _half_step(x, mod_ref, k + 1, w_in_ref, w_out_ref, g_ref, b_ref, alpha)


def _ffn(x, mod, k, w_in, w_out, g, b, *, alpha, tm=1024, mixer=None):
    bsz, length, d = x.shape
    hidden = w_out.shape[0]
    tile = pl.BlockSpec((1, tm, d), lambda i, j: (i, j, 0))
    row = _resident((1, d), lambda i, j: (0, 0))
    ffn_specs = [_resident((d, 2 * hidden), lambda i, j: (0, 0)), _resident((hidden, d), lambda i, j: (0, 0)), row, row]
    ffn_args = (w_in, w_out, g.reshape(1, d), b.reshape(1, d))
    mod_spec = pl.BlockSpec((1, N_MOD, d), _mod_index(mod))
    if mixer is None:
        body = functools.partial(_ffn_body, k=k, alpha=alpha)
        in_specs, args = [tile, mod_spec] + ffn_specs, (x, mod) + ffn_args
    else:
        y, w_proj, g_proj, b_proj = mixer
        kdim = y.shape[2]
        body = functools.partial(_proj_ffn_body, k=k - 1, alpha=alpha)
        in_specs = [tile, pl.BlockSpec((1, tm, kdim), lambda i, j: (i, j, 0)), mod_spec,
                    _resident((kdim, d), lambda i, j: (0, 0)), row, row] + ffn_specs
        args = (x, y, mod, w_proj, g_proj.reshape(1, d), b_proj.reshape(1, d)) + ffn_args
    return pl.pallas_call(
        body,
        grid=(bsz, length // tm),
        in_specs=in_specs,
        out_specs=tile,
        out_shape=jax.ShapeDtypeStruct(x.shape, x.dtype),
        compiler_params=_params("arbitrary", "arbitrary"),
        name="swiglu_half_step" if mixer is None else "mixer_proj_swiglu_half_step",
    )(*args)


HG_STREAMS = (BF16, BF16, BF16, BF16, BF16, F32, F32)


def _hgrn_proj_body(x_ref, mod_ref, w_ref, lbraw_ref, q_ref, v_ref, g_ref, kf_ref, kb_ref, lff_ref, lfb_ref,
                    *, layer):
    d = x_ref.shape[2]
    shift, scale, _gate = _mod_rows(mod_ref, 1)
    h = (x_ref[0] * (1.0 + scale) + shift).astype(BF16)

    def proj(s):
        return _dot(h, w_ref[:, s * d:(s + 1) * d])

    pq = proj(0)
    q_ref[0] = (pq * jax.nn.sigmoid(pq)).astype(BF16)
    v_ref[0] = proj(1).astype(BF16)
    for direction, k_ref, lf_ref in ((0, kf_ref, lff_ref), (1, kb_ref, lfb_ref)):
        raw = lbraw_ref[direction]
        e = jnp.exp(raw - jnp.max(raw, axis=0, keepdims=True))
        lb = jnp.sum(e[:layer + 1, :], axis=0, keepdims=True) / jnp.sum(e, axis=0, keepdims=True)
        f = lb + (1.0 - lb) * jax.nn.sigmoid(proj(2 + direction))
        k_ref[0] = (1.0 - f).astype(BF16)
        lf_ref[0] = jnp.log(f)
    g_ref[0] = proj(4).astype(BF16)


def _hgrn_project(x, mod, w_in, lb_raw, *, layer, tm=512):
    bsz, length, d = x.shape
    tile = pl.BlockSpec((1, tm, d), lambda i, j: (i, j, 0))
    return pl.pallas_call(
        functools.partial(_hgrn_proj_body, layer=layer),
        grid=(bsz, length // tm),
        in_specs=[
            tile,
            pl.BlockSpec((1, N_MOD, d), _mod_index(mod)),
            _resident(w_in.shape, lambda i, j: (0, 0)),
            _resident(lb_raw.shape, lambda i, j: (0, 0, 0)),
        ],
        out_specs=[tile] * len(HG_STREAMS),
        out_shape=[jax.ShapeDtypeStruct(x.shape, dt) for dt in HG_STREAMS],
        compiler_params=_params("arbitrary", "arbitrary"),
        name="hgrn2_project",
    )(x, mod, w_in, lb_raw)


def _split2(v):
    hi = v.astype(BF16)
    lo = (v - hi.astype(F32)).astype(BF16)
    return hi, lo


def _hgrn_scan_body(*refs):
    ns = len(HG_STREAMS)
    ctx_in, x_in = refs[0:ns], refs[ns:2 * ns]
    nw_ref, ox_ref, oc_ref = refs[2 * ns:2 * ns + 3]
    v_scr, m_scr, qin_scr, kin_scr, kvt_scr, ste_scr, vec_scr, o_scr = refs[2 * ns + 3:]
    lc, lx = ctx_in[0].shape[1], x_in[0].shape[1]
    dk = x_in[0].shape[2]
    c = SCAN_CHUNK
    n_ctx, n_all = lc // c, (lc + lx) // c
    mid_f, mid_b = c // 2 - 1, c // 2

    row = lax.broadcasted_iota(jnp.int32, (c, c), 0)
    col = lax.broadcasted_iota(jnp.int32, (c, c), 1)
    keep_f = col <= row
    keep_b = col >= row
    tri = jnp.where(keep_f, 1.0, 0.0).astype(BF16)

    def chunk_rows(n):
        return pl.ds(pl.multiple_of(n * c, c), c)

    def for_segments(fn, unroll):
        for ins, out_ref, first, count in ((ctx_in, oc_ref, 0, n_ctx), (x_in, ox_ref, n_ctx, n_all - n_ctx)):
            def body(j, carry, ins=ins, out_ref=out_ref, first=first):
                fn(ins, out_ref, j, first + j)
                return carry
            lax.fori_loop(0, count, body, 0, unroll=min(unroll, count))

    def decays(ins, _out, j, n):
        q_ref, v_ref, _g, kf_ref, kb_ref, lff_ref, lfb_ref = ins
        src, rows = chunk_rows(j), chunk_rows(n)
        lf_f, lf_b = lff_ref[0, src, :], lfb_ref[0, src, :]
        prefix = _dot(tri, jnp.concatenate(_split2(lf_f) + _split2(lf_b), axis=1))
        b_f = prefix[:, 0:dk] + prefix[:, dk:2 * dk]
        p_b = prefix[:, 2 * dk:3 * dk] + prefix[:, 3 * dk:4 * dk]
        b_b = p_b[c - 1:c, :] - p_b + lf_b
        q = q_ref[0, src, :].astype(F32)
        v_scr[rows, :] = v_ref[0, src, :]
        for lo, b, r_row, end_row, k_ref in ((0, b_f, mid_f, c - 1, kf_ref), (dk, b_b, mid_b, 0, kb_ref)):
            r = b[r_row:r_row + 1, :]
            b_end = b[end_row:end_row + 1, :]
            qin_scr[rows, lo:lo + dk] = (q * jnp.exp(b - r)).astype(BF16)
            kin_scr[rows, lo:lo + dk] = (k_ref[0, src, :].astype(F32) * jnp.exp(r - b)).astype(BF16)
            vec_scr[n, :, lo:lo + dk] = jnp.concatenate(
                [jnp.exp(r), jnp.exp(b_end - r), jnp.exp(b_end), jnp.zeros((5, dk), F32)], axis=0)

    for_segments(decays, 4)

    def intra(n, carry):
        rows = chunk_rows(n)
        s_f = _dot_nt(qin_scr[rows, 0:dk], kin_scr[rows, 0:dk])
        s_b = _dot_nt(qin_scr[rows, dk:2 * dk], kin_scr[rows, dk:2 * dk])
        m_scr[rows, :] = (jnp.where(keep_f, s_f, 0.0) + jnp.where(keep_b, s_b, 0.0)).astype(BF16)
        kvt_scr[n] = _dot_tn(v_scr[rows, :], kin_scr[rows, :])
        return carry

    lax.fori_loop(0, n_all, intra, 0, unroll=6)

    def carry_state(i, carry):
        st_f, st_b = carry
        cb = jnp.where(i < n_ctx, n_ctx - 1 - i, n_all - 1 - (i - n_ctx))
        vec_f, vec_b = vec_scr[i, :, 0:dk], vec_scr[cb, :, dk:2 * dk]
        ste_scr[i, :, 0:dk] = (st_f * vec_f[0:1, :]).astype(BF16)
        ste_scr[cb, :, dk:2 * dk] = (st_b * vec_b[0:1, :]).astype(BF16)
        st_f = vec_f[2:3, :] * st_f + vec_f[1:2, :] * kvt_scr[i, :, 0:dk]
        st_b = vec_b[2:3, :] * st_b + vec_b[1:2, :] * kvt_scr[cb, :, dk:2 * dk]
        return st_f, st_b

    zero = jnp.zeros((dk, dk), F32)
    lax.fori_loop(0, n_all, carry_state, (zero, zero), unroll=6)

    def readout(n, carry):
        rows = chunk_rows(n)
        o_scr[rows, :] = _dot(m_scr[rows, :], v_scr[rows, :]) + _dot_nt(qin_scr[rows, :], ste_scr[n])
        return carry

    lax.fori_loop(0, n_all, readout, 0, unroll=6)

    def norm_gate(ins, out_ref, j, n):
        src = chunk_rows(j)
        o = o_scr[chunk_rows(n), :]
        o = o * lax.rsqrt(jnp.mean(o * o, axis=-1, keepdims=True) + RMS_EPS) * nw_ref[...]
        g = ins[2][0, src, :].astype(F32)
        out_ref[0, src, :] = (o * (g * jax.nn.sigmoid(g))).astype(BF16)

    for_segments(norm_gate, 4)


def _hgrn_scan(ctx_streams, x_streams, norm_w):
    bsz, lx, d = x_streams[0].shape
    lc = ctx_streams[0].shape[1]
    dk = d // HG_HEADS
    total = lc + lx
    n_all = total // SCAN_CHUNK
    head = lambda length: pl.BlockSpec((1, length, dk), lambda b, h: (b, 0, h))
    seq = lambda width: pltpu.VMEM((total, width), BF16)
    return pl.pallas_call(
        _hgrn_scan_body,
        grid=(bsz, HG_HEADS),
        in_specs=[head(lc)] * len(ctx_streams) + [head(lx)] * len(x_streams) + [_resident((1, dk), lambda b, h: (0, 0))],
        out_specs=[head(lx), head(lc)],
        out_shape=[jax.ShapeDtypeStruct((bsz, lx, d), BF16), jax.ShapeDtypeStruct((bsz, lc, d), BF16)],
        scratch_shapes=[
            seq(dk), seq(dk),
            seq(2 * dk), seq(2 * dk),
            pltpu.VMEM((n_all, dk, 2 * dk), F32),
            pltpu.VMEM((n_all, dk, 2 * dk), BF16),
            pltpu.VMEM((n_all, 8, 2 * dk), F32),
            pltpu.VMEM((total, dk), F32),
        ],
        compiler_params=_params("arbitrary", "arbitrary"),
        name="hgrn2_scan",
    )(*ctx_streams, *x_streams, norm_w.reshape(1, dk))


def _chunk_mlp_body(x_ref, mod_ref, w_in_ref, vg_ref, vb_ref, ws_ref, bs_ref, w_out_ref, g_ref, b_ref, o_ref,
                    *, k, alpha, sub):
    inner = w_out_ref.shape[0]
    groups, chunk, _ = ws_ref.shape
    gw = inner // groups
    pair = 2 if groups % 2 == 0 else 1
    tm = x_ref.shape[1]
    shift, scale, gate = _mod_rows(mod_ref, k)
    for s0 in range(0, tm, sub):
        x = x_ref[0, s0:s0 + sub, :]
        h = (x * (1.0 + scale) + shift).astype(BF16)
        v = jax.nn.gelu(_dot(h, w_in_ref[:, inner:2 * inner]), approximate=True)
        v = _layer_norm(v, vg_ref[...], vb_ref[...]).astype(BF16)
        y = None
        for g0 in range(0, groups, pair):
            c0, cw = g0 * gw, pair * gw
            u = jax.nn.gelu(_dot(h, w_in_ref[:, c0:c0 + cw]), approximate=True)
            pieces = []
            for r0 in range(0, sub, chunk):
                cols = []
                for gi in range(g0, g0 + pair):
                    sv = _dot(ws_ref[gi], v[r0:r0 + chunk, gi * gw:(gi + 1) * gw]) + bs_ref[gi]
                    cols.append((u[r0:r0 + chunk, (gi - g0) * gw:(gi - g0 + 1) * gw] * sv).astype(BF16))
                pieces.append(jnp.concatenate(cols, axis=1))
            part = _dot(jnp.concatenate(pieces, axis=0), w_out_ref[c0:c0 + cw, :])
            y = part if y is None else y + part
        o_ref[0, s0:s0 + sub, :] = _layer_norm(alpha * x + gate * y, g_ref[...], b_ref[...])


def _chunk_mlp(x, mod, k, w_in, v_g, v_b, w_s, b_s, w_out, g, b, *, alpha, tm=512, sub=256):
    bsz, length, d = x.shape
    inner = w_out.shape[0]
    groups, chunk, _ = w_s.shape
    return pl.pallas_call(
        functools.partial(_chunk_mlp_body, k=k, alpha=alpha, sub=sub),
        grid=(bsz, length // tm),
        in_specs=[
            pl.BlockSpec((1, tm, d), lambda i, j: (i, j, 0)),
            pl.BlockSpec((1, N_MOD, d), _mod_index(mod)),
            _resident((d, 2 * inner), lambda i, j: (0, 0)),
            _resident((1, inner), lambda i, j: (0, 0)),
            _resident((1, inner), lambda i, j: (0, 0)),
            _resident((groups, chunk, chunk), lambda i, j: (0, 0, 0)),
            _resident((groups, chunk, 1), lambda i, j: (0, 0, 0)),
            _resident((inner, d), lambda i, j: (0, 0)),
            _resident((1, d), lambda i, j: (0, 0)),
            _resident((1, d), lambda i, j: (0, 0)),
        ],
        out_specs=pl.BlockSpec((1, tm, d), lambda i, j: (i, j, 0)),
        out_shape=jax.ShapeDtypeStruct(x.shape, x.dtype),
        compiler_params=_params("arbitrary", "arbitrary"),
        name="chunk_mlp",
    )(x, mod, w_in, v_g.reshape(1, inner), v_b.reshape(1, inner), w_s, b_s.reshape(groups, chunk, 1), w_out,
      g.reshape(1, d), b.reshape(1, d))


def kernel(x, c, ctx, c_ctx, mod_w, mod_b, ln_g, ln_b, ffn_w_in, ffn_w_out, hg_w_in, hg_lower_bounds, hg_norm_w,
           hg_w_out, cm_w_in, cm_v_g, cm_v_b, cm_w_s, cm_b_s, cm_w_out):
    depth = mod_w.shape[0]
    bsz, _, d = x.shape
    alpha = (2 * depth) ** 0.25
    assert CM_CHUNK % GRID_W == 0 and x.shape[1] % CM_CHUNK == 0 and ctx.shape[1] % CM_CHUNK == 0

    pad = (-(bsz + 1)) % 8
    cond = jnp.concatenate([c, c_ctx[None, :], jnp.zeros((pad, d), c.dtype)], axis=0)
    m_all = _modulation(cond, mod_w, mod_b)

    ffn_w_in16, ffn_w_out16 = ffn_w_in.astype(BF16), ffn_w_out.astype(BF16)
    hg_w_in16, hg_w_out16 = hg_w_in.astype(BF16), hg_w_out.astype(BF16)
    lb_raw = hg_lower_bounds.astype(F32)
    cm_w_in16, cm_w_out16, cm_w_s16 = cm_w_in.astype(BF16), cm_w_out.astype(BF16), cm_w_s.astype(BF16)

    ctx_shape = ctx.shape
    flat = lambda t: t.reshape(1, -1, d)
    for i in range(depth):
        last = i == depth - 1
        kind = i % N_MIXERS
        j = i // N_MIXERS
        ctx_needed = (not last) or kind == 0
        mx = m_all[i, :bsz].reshape(bsz, N_MOD, d)
        mc = m_all[i, bsz].reshape(1, N_MOD, d)
        ffn = functools.partial(_ffn, alpha=alpha)
        x = ffn(x, mx, 0, ffn_w_in16[i, 0], ffn_w_out16[i, 0], ln_g[i, 0], ln_b[i, 0])
        if ctx_needed:
            ctx = ffn(flat(ctx), mc, 0, ffn_w_in16[i, 0], ffn_w_out16[i, 0], ln_g[i, 0], ln_b[i, 0]).reshape(ctx_shape)
        if kind == 0:
            x_streams = _hgrn_project(x, mx, hg_w_in16[j], lb_raw, layer=i)
            ctx_streams = [s.reshape(ctx_shape) for s in _hgrn_project(flat(ctx), mc, hg_w_in16[j], lb_raw, layer=i)]
            ox, oc = _hgrn_scan(ctx_streams, x_streams, hg_norm_w[j])
            mixer_x = (ox, hg_w_out16[j], ln_g[i, 1], ln_b[i, 1])
            mixer_c = (flat(oc), hg_w_out16[j], ln_g[i, 1], ln_b[i, 1])
        else:
            mixer_x = mixer_c = None
            cm = functools.partial(_chunk_mlp, alpha=alpha)
            x = cm(x, mx, 1, cm_w_in16[j], cm_v_g[j], cm_v_b[j], cm_w_s16[j], cm_b_s[j], cm_w_out16[j],
                   ln_g[i, 1], ln_b[i, 1])
            if not last:
                ctx = cm(ctx, mc, 1, cm_w_in16[j], cm_v_g[j], cm_v_b[j], cm_w_s16[j], cm_b_s[j], cm_w_out16[j],
                         ln_g[i, 1], ln_b[i, 1])
        x = ffn(x, mx, 2, ffn_w_in16[i, 1], ffn_w_out16[i, 1], ln_g[i, 2], ln_b[i, 2], mixer=mixer_x)
        if not last:
            ctx = ffn(flat(ctx), mc, 2, ffn_w_in16[i, 1], ffn_w_out16[i, 1], ln_g[i, 2], ln_b[i, 2],
                      mixer=mixer_c).reshape(ctx_shape)
    return x
```

```python
import functools

import jax
import jax.numpy as jnp
from jax import lax
from jax.experimental import pallas as pl
from jax.experimental.pallas import tpu as pltpu

GRID_W = 64
N_MIXERS = 2
HG_HEADS = 8
CM_CHUNK = 128
CM_GROUPS = 8
LN_EPS = 1e-5
RMS_EPS = 1e-6
N_MOD = 9

V7X_VMEM_BYTES = 64 * 1024 * 1024
VMEM_LIMIT_BYTES = V7X_VMEM_BYTES - 8 * 1024 * 1024
MXU_DIM = 256
LANES, SUBLANES = 128, 8
SCAN_CHUNK = 128

F32 = jnp.float32
BF16 = jnp.bfloat16


def _params(*semantics):
    return pltpu.CompilerParams(dimension_semantics=semantics, vmem_limit_bytes=VMEM_LIMIT_BYTES)


def _resident(shape, index_map):
    return pl.BlockSpec(shape, index_map, pipeline_mode=pl.Buffered(1))


def _odd_pitch(w):
    tiles = w.shape[1] // LANES
    return jnp.pad(w, ((0, 0), (0, LANES))) if tiles % SUBLANES == 0 else w


def _silu(x):
    half = 0.5 * x
    return half + half * jnp.tanh(half)


def _layer_norm(z, g, b):
    mu = jnp.mean(z, axis=-1, keepdims=True)
    zc = z - mu
    var = jnp.mean(zc * zc, axis=-1, keepdims=True)
    return zc * lax.rsqrt(var + LN_EPS) * g + b


def _mod_rows(mod_ref, k):
    return mod_ref[0, 3 * k:3 * k + 1, :], mod_ref[0, 3 * k + 1:3 * k + 2, :], mod_ref[0, 3 * k + 2:3 * k + 3, :]


def _mod_index(mod):
    return (lambda i, j: (i, 0, 0)) if mod.shape[0] > 1 else (lambda i, j: (0, 0, 0))


def _dot(a, b):
    return jnp.dot(a, b, preferred_element_type=F32)


def _dot_nt(a, b):
    return lax.dot_general(a, b, (((1,), (1,)), ((), ())), preferred_element_type=F32)


def _dot_tn(a, b):
    return lax.dot_general(a, b, (((0,), (0,)), ((), ())), preferred_element_type=F32)


def _modulation_body(cond_ref, w_ref, b_ref, o_ref):
    cond = cond_ref[...]
    act = _silu(cond).astype(BF16)
    o_ref[0] = _dot(act, w_ref[0].astype(BF16)) + b_ref[0]


def _modulation(cond, mod_w, mod_b):
    depth, d, n = mod_w.shape
    r = cond.shape[0]
    tn = d
    return pl.pallas_call(
        _modulation_body,
        grid=(depth, n // tn),
        in_specs=[
            pl.BlockSpec((r, d), lambda i, j: (0, 0)),
            pl.BlockSpec((1, d, tn), lambda i, j: (i, 0, j)),
            pl.BlockSpec((1, 1, tn), lambda i, j: (i, 0, j)),
        ],
        out_specs=pl.BlockSpec((1, r, tn), lambda i, j: (i, 0, j)),
        out_shape=jax.ShapeDtypeStruct((depth, r, n), F32),
        compiler_params=_params("arbitrary", "arbitrary"),
        name="modulation",
    )(cond, mod_w, mod_b.reshape(depth, 1, n))


def _hidden_chunks(hidden):
    step = 4 * MXU_DIM
    return [(c0, min(step, hidden - c0)) for c0 in range(0, hidden, step)]


def _half_step(x, mod_ref, k, w_in_ref, w_out_ref, g_ref, b_ref, alpha):
    hidden = w_out_ref.shape[0]
    shift, scale, gate = _mod_rows(mod_ref, k)
    h = (x * (1.0 + scale) + shift).astype(BF16)
    acc = None
    for c0, cw in _hidden_chunks(hidden):
        gt = _dot(h, w_in_ref[:, c0:c0 + cw])
        up = _dot(h, w_in_ref[:, hidden + c0:hidden + c0 + cw])
        a = (_silu(gt) * up).astype(BF16)
        part = _dot(a, w_out_ref[c0:c0 + cw, :])
        acc = part if acc is None else acc + part
    z = alpha * x + gate * (0.5 * acc)
    return _layer_norm(z, g_ref[...], b_ref[...])


def _ffn_body(x_ref, mod_ref, w_in_ref, w_out_ref, g_ref, b_ref, o_ref, *, k, alpha):
    o_ref[0] = _half_step(x_ref[0], mod_ref, k, w_in_ref, w_out_ref, g_ref, b_ref, alpha)


def _proj_ffn_body(x_ref, y_ref, mod_ref, wp_ref, gp_ref, bp_ref, w_in_ref, w_out_ref, g_ref, b_ref, o_ref,
                   *, k, alpha):
    _s, _c, gate = _mod_rows(mod_ref, k)
    x = _layer_norm(alpha * x_ref[0] + gate * _dot(y_ref[0], wp_ref[...]), gp_ref[...], bp_ref[...])
    o_ref[0] = _half_step(x, mod_ref, k + 1, w_in_ref, w_out_ref, g_ref, b_ref, alpha)


def _ffn(x, mod, k, w_in, w_out, g, b, *, alpha, tm=1024, mixer=None):
    bsz, length, d = x.shape
    hidden = w_out.shape[0]
    tile = pl.BlockSpec((1, tm, d), lambda i, j: (i, j, 0))
    row = _resident((1, d), lambda i, j: (0, 0))
    ffn_specs = [_resident(w_in.shape, lambda i, j: (0, 0)), _resident(w_out.shape, lambda i, j: (0, 0)), row, row]
    ffn_args = (w_in, w_out, g.reshape(1, d), b.reshape(1, d))
    mod_spec = pl.BlockSpec((1, N_MOD, d), _mod_index(mod))
    if mixer is None:
        body = functools.partial(_ffn_body, k=k, alpha=alpha)
        in_specs, args = [tile, mod_spec] + ffn_specs, (x, mod) + ffn_args
    else:
        y, w_proj, g_proj, b_proj = mixer
        kdim = y.shape[2]
        body = functools.partial(_proj_ffn_body, k=k - 1, alpha=alpha)
        in_specs = [tile, pl.BlockSpec((1, tm, kdim), lambda i, j: (i, j, 0)), mod_spec,
                    _resident(w_proj.shape, lambda i, j: (0, 0)), row, row] + ffn_specs
        args = (x, y, mod, w_proj, g_proj.reshape(1, d), b_proj.reshape(1, d)) + ffn_args
    return pl.pallas_call(
        body,
        grid=(bsz, length // tm),
        in_specs=in_specs,
        out_specs=tile,
        out_shape=jax.ShapeDtypeStruct(x.shape, x.dtype),
        compiler_params=_params("arbitrary", "arbitrary"),
        name="swiglu_half_step" if mixer is None else "mixer_proj_swiglu_half_step",
    )(*args)


HG_STREAMS = (BF16, BF16, BF16, BF16, BF16, F32, F32)


def _hgrn_proj_body(x_ref, mod_ref, w_ref, lbraw_ref, q_ref, v_ref, g_ref, kf_ref, kb_ref, lff_ref, lfb_ref,
                    *, layer):
    d = x_ref.shape[2]
    shift, scale, _gate = _mod_rows(mod_ref, 1)
    h = (x_ref[0] * (1.0 + scale) + shift).astype(BF16)

    def proj(s):
        return _dot(h, w_ref[:, s * d:(s + 1) * d])

    pq = proj(0)
    q_ref[0] = _silu(pq).astype(BF16)
    v_ref[0] = proj(1).astype(BF16)
    for direction, k_ref, lf_ref in ((0, kf_ref, lff_ref), (1, kb_ref, lfb_ref)):
        raw = lbraw_ref[direction]
        e = jnp.exp(raw - jnp.max(raw, axis=0, keepdims=True))
        lb = jnp.sum(e[:layer + 1, :], axis=0, keepdims=True) / jnp.sum(e, axis=0, keepdims=True)
        half_span = 0.5 * (1.0 - lb)
        swing = half_span * jnp.tanh(0.5 * proj(2 + direction))
        k_ref[0] = (half_span - swing).astype(BF16)
        lf_ref[0] = jnp.log((lb + half_span) + swing)
    g_ref[0] = proj(4).astype(BF16)


def _hgrn_project(x, mod, w_in, lb_raw, *, layer, tm=512):
    bsz, length, d = x.shape
    tile = pl.BlockSpec((1, tm, d), lambda i, j: (i, j, 0))
    return pl.pallas_call(
        functools.partial(_hgrn_proj_body, layer=layer),
        grid=(bsz, length // tm),
        in_specs=[
            tile,
            pl.BlockSpec((1, N_MOD, d), _mod_index(mod)),
            _resident(w_in.shape, lambda i, j: (0, 0)),
            _resident(lb_raw.shape, lambda i, j: (0, 0, 0)),
        ],
        out_specs=[tile] * len(HG_STREAMS),
        out_shape=[jax.ShapeDtypeStruct(x.shape, dt) for dt in HG_STREAMS],
        compiler_params=_params("arbitrary", "arbitrary"),
        name="hgrn2_project",
    )(x, mod, w_in, lb_raw)


def _split2(v):
    hi = v.astype(BF16)
    lo = (v - hi.astype(F32)).astype(BF16)
    return hi, lo


def _hgrn_scan_body(*refs):
    ns = len(HG_STREAMS)
    ctx_in, x_in = refs[0:ns], refs[ns:2 * ns]
    nw_ref, ox_ref, oc_ref = refs[2 * ns:2 * ns + 3]
    v_scr, m_scr, qin_scr, kin_scr, kvt_scr, ste_scr, vec_scr, o_scr = refs[2 * ns + 3:]
    lc, lx = ctx_in[0].shape[1], x_in[0].shape[1]
    dk = x_in[0].shape[2]
    c = SCAN_CHUNK
    n_ctx, n_all = lc // c, (lc + lx) // c
    mid_f, mid_b = c // 2 - 1, c // 2

    row = lax.broadcasted_iota(jnp.int32, (c, c), 0)
    col = lax.broadcasted_iota(jnp.int32, (c, c), 1)
    keep_f = col <= row
    keep_b = col >= row
    tri = jnp.where(keep_f, 1.0, 0.0).astype(BF16)

    def chunk_rows(n):
        return pl.ds(pl.multiple_of(n * c, c), c)

    def for_segments(fn, unroll):
        for ins, out_ref, first, count in ((ctx_in, oc_ref, 0, n_ctx), (x_in, ox_ref, n_ctx, n_all - n_ctx)):
            def body(j, carry, ins=ins, out_ref=out_ref, first=first):
                fn(ins, out_ref, j, first + j)
                return carry
            lax.fori_loop(0, count, body, 0, unroll=min(unroll, count))

    def decays(ins, _out, j, n):
        q_ref, v_ref, _g, kf_ref, kb_ref, lff_ref, lfb_ref = ins
        src, rows = chunk_rows(j), chunk_rows(n)
        q = q_ref[0, src, :].astype(F32)
        v_scr[rows, :] = v_ref[0, src, :]
        for lo, lf_ref, r_row, end_row, k_ref in ((0, lff_ref, mid_f, c - 1, kf_ref), (dk, lfb_ref, mid_b, 0, kb_ref)):
            lf = lf_ref[0, src, :]
            prefix = _dot(tri, jnp.concatenate(_split2(lf), axis=1))
            b = prefix[:, 0:dk] + prefix[:, dk:2 * dk]
            if lo:
                b = b[c - 1:c, :] - b + lf
            r = b[r_row:r_row + 1, :]
            b_end = b[end_row:end_row + 1, :]
            qin_scr[rows, lo:lo + dk] = (q * jnp.exp(b - r)).astype(BF16)
            kin_scr[rows, lo:lo + dk] = (k_ref[0, src, :].astype(F32) * jnp.exp(r - b)).astype(BF16)
            vec_scr[n, :, lo:lo + dk] = jnp.concatenate(
                [jnp.exp(r), jnp.exp(b_end - r), jnp.exp(b_end), jnp.zeros((5, dk), F32)], axis=0)

    for_segments(decays, 4)

    def intra(n, carry):
        rows = chunk_rows(n)
        s_f = _dot_nt(qin_scr[rows, 0:dk], kin_scr[rows, 0:dk])
        s_b = _dot_nt(qin_scr[rows, dk:2 * dk], kin_scr[rows, dk:2 * dk])
        m_scr[rows, :] = (jnp.where(keep_f, s_f, 0.0) + jnp.where(keep_b, s_b, 0.0)).astype(BF16)
        kvt_scr[n] = _dot_tn(v_scr[rows, :], kin_scr[rows, :])
        return carry

    lax.fori_loop(0, n_all, intra, 0, unroll=6)

    def carry_state(i, carry):
        st_f, st_b = carry
        cb = jnp.where(i < n_ctx, n_ctx - 1 - i, n_all - 1 - (i - n_ctx))
        vec_f, vec_b = vec_scr[i, :, 0:dk], vec_scr[cb, :, dk:2 * dk]
        ste_scr[i, :, 0:dk] = (st_f * vec_f[0:1, :]).astype(BF16)
        ste_scr[cb, :, dk:2 * dk] = (st_b * vec_b[0:1, :]).astype(BF16)
        st_f = vec_f[2:3, :] * st_f + vec_f[1:2, :] * kvt_scr[i, :, 0:dk]
        st_b = vec_b[2:3, :] * st_b + vec_b[1:2, :] * kvt_scr[cb, :, dk:2 * dk]
        return st_f, st_b

    zero = jnp.zeros((dk, dk), F32)
    lax.fori_loop(0, n_all, carry_state, (zero, zero), unroll=6)

    def readout(n, carry):
        rows = chunk_rows(n)
        o_scr[rows, :] = _dot(m_scr[rows, :], v_scr[rows, :]) + _dot_nt(qin_scr[rows, :], ste_scr[n])
        return carry

    lax.fori_loop(0, n_all, readout, 0, unroll=6)

    def norm_gate(ins, out_ref, j, n):
        src = chunk_rows(j)
        o = o_scr[chunk_rows(n), :]
        o = o * lax.rsqrt(jnp.mean(o * o, axis=-1, keepdims=True) + RMS_EPS) * nw_ref[...]
        g = ins[2][0, src, :].astype(F32)
        out_ref[0, src, :] = (o * _silu(g)).astype(BF16)

    for_segments(norm_gate, 4)


def _hgrn_scan(ctx_streams, x_streams, norm_w):
    bsz, lx, d = x_streams[0].shape
    lc = ctx_streams[0].shape[1]
    dk = d // HG_HEADS
    total = lc + lx
    n_all = total // SCAN_CHUNK
    head = lambda length: pl.BlockSpec((1, length, dk), lambda b, h: (b, 0, h))
    seq = lambda width: pltpu.VMEM((total, width), BF16)
    return pl.pallas_call(
        _hgrn_scan_body,
        grid=(bsz, HG_HEADS),
        in_specs=[head(lc)] * len(ctx_streams) + [head(lx)] * len(x_streams) + [_resident((1, dk), lambda b, h: (0, 0))],
        out_specs=[head(lx), head(lc)],
        out_shape=[jax.ShapeDtypeStruct((bsz, lx, d), BF16), jax.ShapeDtypeStruct((bsz, lc, d), BF16)],
        scratch_shapes=[
            seq(dk), seq(dk),
            seq(2 * dk), seq(2 * dk),
            pltpu.VMEM((n_all, dk, 2 * dk), F32),
            pltpu.VMEM((n_all, dk, 2 * dk), BF16),
            pltpu.VMEM((n_all, 8, 2 * dk), F32),
            pltpu.VMEM((total, dk), F32),
        ],
        compiler_params=_params("arbitrary", "arbitrary"),
        name="hgrn2_scan",
    )(*ctx_streams, *x_streams, norm_w.reshape(1, dk))


GELU_C1 = (2.0 / jnp.pi) ** 0.5
GELU_C2 = 0.044715 * GELU_C1


def _gelu_tanh(x):
    half = 0.5 * x
    return half + half * jnp.tanh(x * (GELU_C1 + GELU_C2 * (x * x)))

def _chunk_mlp_body(x_ref, mod_ref, w_in_ref, vg_ref, vb_ref, ws_ref, bs_ref, w_out_ref, g_ref, b_ref, o_ref,
                    *, k, alpha, sub):
    inner = w_out_ref.shape[0]
    groups, chunk, _ = ws_ref.shape
    gw = inner // groups
    pair = 2 if groups % 2 == 0 else 1
    tm, d = x_ref.shape[1:]
    shift, scale, gate = _mod_rows(mod_ref, k)
    for s0 in range(0, tm, sub):
        x = x_ref[0, s0:s0 + sub, :]
        h = (x * (1.0 + scale) + shift).astype(BF16)
        v = _gelu_tanh(_dot(h, w_in_ref[:, inner:2 * inner]))
        v = _layer_norm(v, vg_ref[...], vb_ref[...]).astype(BF16)
        y = None
        for g0 in range(0, groups, pair):
            c0, cw = g0 * gw, pair * gw
            u = _gelu_tanh(_dot(h, w_in_ref[:, c0:c0 + cw]))
            pieces = []
            for r0 in range(0, sub, chunk):
                cols = []
                for gi in range(g0, g0 + pair):
                    sv = _dot(ws_ref[gi], v[r0:r0 + chunk, gi * gw:(gi + 1) * gw]) + bs_ref[gi]
                    cols.append((u[r0:r0 + chunk, (gi - g0) * gw:(gi - g0 + 1) * gw] * sv).astype(BF16))
                pieces.append(jnp.concatenate(cols, axis=1))
            part = _dot(jnp.concatenate(pieces, axis=0), w_out_ref[c0:c0 + cw, 0:d])
            y = part if y is None else y + part
        o_ref[0, s0:s0 + sub, :] = _layer_norm(alpha * x + gate * y, g_ref[...], b_ref[...])


def _chunk_mlp(x, mod, k, w_in, v_g, v_b, w_s, b_s, w_out, g, b, *, alpha, tm=512, sub=256):
    bsz, length, d = x.shape
    inner = w_out.shape[0]
    groups, chunk, _ = w_s.shape
    w_in, w_out = _odd_pitch(w_in), _odd_pitch(w_out)
    return pl.pallas_call(
        functools.partial(_chunk_mlp_body, k=k, alpha=alpha, sub=sub),
        grid=(bsz, length // tm),
        in_specs=[
            pl.BlockSpec((1, tm, d), lambda i, j: (i, j, 0)),
            pl.BlockSpec((1, N_MOD, d), _mod_index(mod)),
            _resident(w_in.shape, lambda i, j: (0, 0)),
            _resident((1, inner), lambda i, j: (0, 0)),
            _resident((1, inner), lambda i, j: (0, 0)),
            _resident((groups, chunk, chunk), lambda i, j: (0, 0, 0)),
            _resident((groups, chunk, 1), lambda i, j: (0, 0, 0)),
            _resident(w_out.shape, lambda i, j: (0, 0)),
            _resident((1, d), lambda i, j: (0, 0)),
            _resident((1, d), lambda i, j: (0, 0)),
        ],
        out_specs=pl.BlockSpec((1, tm, d), lambda i, j: (i, j, 0)),
        out_shape=jax.ShapeDtypeStruct(x.shape, x.dtype),
        compiler_params=_params("arbitrary", "arbitrary"),
        name="chunk_mlp",
    )(x, mod, w_in, v_g.reshape(1, inner), v_b.reshape(1, inner), w_s, b_s.reshape(groups, chunk, 1), w_out,
      g.reshape(1, d), b.reshape(1, d))


def kernel(x, c, ctx, c_ctx, mod_w, mod_b, ln_g, ln_b, ffn_w_in, ffn_w_out, hg_w_in, hg_lower_bounds, hg_norm_w,
           hg_w_out, cm_w_in, cm_v_g, cm_v_b, cm_w_s, cm_b_s, cm_w_out):
    depth = mod_w.shape[0]
    bsz, _, d = x.shape
    alpha = (2 * depth) ** 0.25
    assert CM_CHUNK % GRID_W == 0 and x.shape[1] % CM_CHUNK == 0 and ctx.shape[1] % CM_CHUNK == 0

    pad = (-(bsz + 1)) % 8
    cond = jnp.concatenate([c, c_ctx[None, :], jnp.zeros((pad, d), c.dtype)], axis=0)
    m_all = _modulation(cond, mod_w, mod_b)

    ffn_w_in16, ffn_w_out16 = ffn_w_in.astype(BF16), ffn_w_out.astype(BF16)
    hg_w_in16, hg_w_out16 = hg_w_in.astype(BF16), hg_w_out.astype(BF16)
    lb_raw = hg_lower_bounds.astype(F32)
    cm_w_in16, cm_w_out16, cm_w_s16 = cm_w_in.astype(BF16), cm_w_out.astype(BF16), cm_w_s.astype(BF16)

    ctx_shape = ctx.shape
    flat = lambda t: t.reshape(1, -1, d)
    for i in range(depth):
        last = i == depth - 1
        kind = i % N_MIXERS
        j = i // N_MIXERS
        ctx_needed = (not last) or kind == 0
        mx = m_all[i, :bsz].reshape(bsz, N_MOD, d)
        mc = m_all[i, bsz].reshape(1, N_MOD, d)
        ffn = functools.partial(_ffn, alpha=alpha)
        x = ffn(x, mx, 0, ffn_w_in16[i, 0], ffn_w_out16[i, 0], ln_g[i, 0], ln_b[i, 0])
        if ctx_needed:
            ctx = ffn(flat(ctx), mc, 0, ffn_w_in16[i, 0], ffn_w_out16[i, 0], ln_g[i, 0], ln_b[i, 0]).reshape(ctx_shape)
        if kind == 0:
            x_streams = _hgrn_project(x, mx, hg_w_in16[j], lb_raw, layer=i)
            ctx_streams = [s.reshape(ctx_shape) for s in _hgrn_project(flat(ctx), mc, hg_w_in16[j], lb_raw, layer=i)]
            ox, oc = _hgrn_scan(ctx_streams, x_streams, hg_norm_w[j])
            mixer_x = (ox, hg_w_out16[j], ln_g[i, 1], ln_b[i, 1])
            mixer_c = (flat(oc), hg_w_out16[j], ln_g[i, 1], ln_b[i, 1])
        else:
            mixer_x = mixer_c = None
            cm = functools.partial(_chunk_mlp, alpha=alpha)
            x = cm(x, mx, 1, cm_w_in16[j], cm_v_g[j], cm_v_b[j], cm_w_s16[j], cm_b_s[j], cm_w_out16[j],
                   ln_g[i, 1], ln_b[i, 1])
            if not last:
                ctx = cm(ctx, mc, 1, cm_w_in16[j], cm_v_g[j], cm_v_b[j], cm_w_s16[j], cm_b_s[j], cm_w_out16[j],
                         ln_g[i, 1], ln_b[i, 1])
        x = ffn(x, mx, 2, ffn_w_in16[i, 1], ffn_w_out16[i, 1], ln_g[i, 2], ln_b[i, 2], mixer=mixer_x)
        if not last:
            ctx = ffn(flat(ctx), mc, 2, ffn_w_in16[i, 1], ffn_w_out16[i, 1], ln_g[i, 2], ln_b[i, 2],
                      mixer=mixer_c).reshape(ctx_shape)
    return x
```

```python
import functools

import jax
import jax.numpy as jnp
from jax import lax
from jax.experimental import pallas as pl
from jax.experimental.pallas import tpu as pltpu

GRID_W = 64
N_MIXERS = 2
HG_HEADS = 8
CM_CHUNK = 128
CM_GROUPS = 8
LN_EPS = 1e-5
RMS_EPS = 1e-6
N_MOD = 9

V7X_VMEM_BYTES = 64 * 1024 * 1024
VMEM_LIMIT_BYTES = V7X_VMEM_BYTES - 8 * 1024 * 1024
MXU_DIM = 256
LANES, SUBLANES = 128, 8
SCAN_CHUNK = 128

F32 = jnp.float32
BF16 = jnp.bfloat16


def _params(*semantics):
    return pltpu.CompilerParams(dimension_semantics=semantics, vmem_limit_bytes=VMEM_LIMIT_BYTES)


def _resident(shape, index_map):
    return pl.BlockSpec(shape, index_map, pipeline_mode=pl.Buffered(1))


def _odd_pitch(w):
    tiles = w.shape[1] // LANES
    return jnp.pad(w, ((0, 0), (0, LANES))) if tiles % SUBLANES == 0 else w


def _silu(x):
    half = 0.5 * x
    return half + half * jnp.tanh(half)


def _layer_norm(z, g, b):
    mu = jnp.mean(z, axis=-1, keepdims=True)
    zc = z - mu
    var = jnp.mean(zc * zc, axis=-1, keepdims=True)
    return zc * lax.rsqrt(var + LN_EPS) * g + b


def _mod_rows(mod_ref, k):
    return mod_ref[0, 3 * k:3 * k + 1, :], mod_ref[0, 3 * k + 1:3 * k + 2, :], mod_ref[0, 3 * k + 2:3 * k + 3, :]


def _mod_index(mod):
    return (lambda i, j: (i, 0, 0)) if mod.shape[0] > 1 else (lambda i, j: (0, 0, 0))


def _dot(a, b):
    return jnp.dot(a, b, preferred_element_type=F32)


def _dot_nt(a, b):
    return lax.dot_general(a, b, (((1,), (1,)), ((), ())), preferred_element_type=F32)


def _dot_tn(a, b):
    return lax.dot_general(a, b, (((0,), (0,)), ((), ())), preferred_element_type=F32)


def _modulation_body(cond_ref, w_ref, b_ref, o_ref):
    cond = cond_ref[...]
    act = _silu(cond).astype(BF16)
    o_ref[0] = _dot(act, w_ref[0].astype(BF16)) + b_ref[0]


def _modulation(cond, mod_w, mod_b):
    depth, d, n = mod_w.shape
    r = cond.shape[0]
    tn = d
    return pl.pallas_call(
        _modulation_body,
        grid=(depth, n // tn),
        in_specs=[
            pl.BlockSpec((r, d), lambda i, j: (0, 0)),
            pl.BlockSpec((1, d, tn), lambda i, j: (i, 0, j)),
            pl.BlockSpec((1, 1, tn), lambda i, j: (i, 0, j)),
        ],
        out_specs=pl.BlockSpec((1, r, tn), lambda i, j: (i, 0, j)),
        out_shape=jax.ShapeDtypeStruct((depth, r, n), F32),
        compiler_params=_params("arbitrary", "arbitrary"),
        name="modulation",
    )(cond, mod_w, mod_b.reshape(depth, 1, n))


def _hidden_chunks(hidden):
    step = 4 * MXU_DIM
    return [(c0, min(step, hidden - c0)) for c0 in range(0, hidden, step)]


def _half_step(x, mod_ref, k, w_in_ref, w_out_ref, g_ref, b_ref, alpha):
    hidden = w_out_ref.shape[0]
    shift, scale, gate = _mod_rows(mod_ref, k)
    h = (x * (1.0 + scale) + shift).astype(BF16)
    acc = None
    for c0, cw in _hidden_chunks(hidden):
        gt = _dot(h, w_in_ref[:, c0:c0 + cw])
        up = _dot(h, w_in_ref[:, hidden + c0:hidden + c0 + cw])
        a = (_silu(gt) * up).astype(BF16)
        part = _dot(a, w_out_ref[c0:c0 + cw, :])
        acc = part if acc is None else acc + part
    z = alpha * x + gate * (0.5 * acc)
    return _layer_norm(z, g_ref[...], b_ref[...])


def _ffn_body(x_ref, mod_ref, w_in_ref, w_out_ref, g_ref, b_ref, o_ref, *, k, alpha):
    o_ref[0] = _half_step(x_ref[0], mod_ref, k, w_in_ref, w_out_ref, g_ref, b_ref, alpha)


def _proj_ffn_body(x_ref, y_ref, mod_ref, wp_ref, gp_ref, bp_ref, w_in_ref, w_out_ref, g_ref, b_ref, o_ref,
                   *, k, alpha):
    _s, _c, gate = _mod_rows(mod_ref, k)
    x = _layer_norm(alpha * x_ref[0] + gate * _dot(y_ref[0], wp_ref[...]), gp_ref[...], bp_ref[...])
    o_ref[0] = _half_step(x, mod_ref, k + 1, w_in_ref, w_out_ref, g_ref, b_ref, alpha)


def _ffn(x, mod, k, w_in, w_out, g, b, *, alpha, tm=1024, mixer=None):
    bsz, length, d = x.shape
    hidden = w_out.shape[0]
    tile = pl.BlockSpec((1, tm, d), lambda i, j: (i, j, 0))
    row = _resident((1, d), lambda i, j: (0, 0))
    ffn_specs = [_resident(w_in.shape, lambda i, j: (0, 0)), _resident(w_out.shape, lambda i, j: (0, 0)), row, row]
    ffn_args = (w_in, w_out, g.reshape(1, d), b.reshape(1, d))
    mod_spec = pl.BlockSpec((1, N_MOD, d), _mod_index(mod))
    if mixer is None:
        body = functools.partial(_ffn_body, k=k, alpha=alpha)
        in_specs, args = [tile, mod_spec] + ffn_specs, (x, mod) + ffn_args
    else:
        y, w_proj, g_proj, b_proj = mixer
        kdim = y.shape[2]
        body = functools.partial(_proj_ffn_body, k=k - 1, alpha=alpha)
        in_specs = [tile, pl.BlockSpec((1, tm, kdim), lambda i, j: (i, j, 0)), mod_spec,
                    _resident(w_proj.shape, lambda i, j: (0, 0)), row, row] + ffn_specs
        args = (x, y, mod, w_proj, g_proj.reshape(1, d), b_proj.reshape(1, d)) + ffn_args
    return pl.pallas_call(
        body,
        grid=(bsz, length // tm),
        in_specs=in_specs,
        out_specs=tile,
        out_shape=jax.ShapeDtypeStruct(x.shape, x.dtype),
        compiler_params=_params("arbitrary", "arbitrary"),
        name="swiglu_half_step" if mixer is None else "mixer_proj_swiglu_half_step",
    )(*args)


HG_ACT_STREAMS, HG_Q, HG_V, HG_G, HG_KF, HG_KB = 5, 0, 1, 2, 3, 4
HG_DIRECTIONS = 2


def _put_per_head(ref, slot, n_slots, value):
    dk = value.shape[1] // HG_HEADS
    for h in range(HG_HEADS):
        ref[0, :, (h * n_slots + slot) * dk:(h * n_slots + slot + 1) * dk] = value[:, h * dk:(h + 1) * dk]


def _hgrn_proj_body(x_ref, mod_ref, w_ref, lbraw_ref, act_ref, lf_ref, *, layer):
    d = x_ref.shape[2]
    shift, scale, _gate = _mod_rows(mod_ref, 1)
    h = (x_ref[0] * (1.0 + scale) + shift).astype(BF16)

    def proj(s):
        return _dot(h, w_ref[:, s * d:(s + 1) * d])

    _put_per_head(act_ref, HG_Q, HG_ACT_STREAMS, _silu(proj(0)).astype(BF16))
    _put_per_head(act_ref, HG_V, HG_ACT_STREAMS, proj(1).astype(BF16))
    for direction, k_slot in ((0, HG_KF), (1, HG_KB)):
        raw = lbraw_ref[direction]
        e = jnp.exp(raw - jnp.max(raw, axis=0, keepdims=True))
        lb = jnp.sum(e[:layer + 1, :], axis=0, keepdims=True) / jnp.sum(e, axis=0, keepdims=True)
        half_span = 0.5 * (1.0 - lb)
        swing = half_span * jnp.tanh(0.5 * proj(2 + direction))
        _put_per_head(act_ref, k_slot, HG_ACT_STREAMS, (half_span - swing).astype(BF16))
        _put_per_head(lf_ref, direction, HG_DIRECTIONS, jnp.log((lb + half_span) + swing))
    _put_per_head(act_ref, HG_G, HG_ACT_STREAMS, proj(4).astype(BF16))


def _hgrn_project(x, mod, w_in, lb_raw, *, layer, tm=512):
    bsz, length, d = x.shape
    tile = lambda width: pl.BlockSpec((1, tm, width), lambda i, j: (i, j, 0))
    w_in = _odd_pitch(w_in)
    return pl.pallas_call(
        functools.partial(_hgrn_proj_body, layer=layer),
        grid=(bsz, length // tm),
        in_specs=[
            tile(d),
            pl.BlockSpec((1, N_MOD, d), _mod_index(mod)),
            _resident(w_in.shape, lambda i, j: (0, 0)),
            _resident(lb_raw.shape, lambda i, j: (0, 0, 0)),
        ],
        out_specs=[tile(HG_ACT_STREAMS * d), tile(HG_DIRECTIONS * d)],
        out_shape=[jax.ShapeDtypeStruct((bsz, length, HG_ACT_STREAMS * d), BF16),
                   jax.ShapeDtypeStruct((bsz, length, HG_DIRECTIONS * d), F32)],
        compiler_params=_params("arbitrary", "arbitrary"),
        name="hgrn2_project",
    )(x, mod, w_in, lb_raw)


def _split2(v):
    hi = v.astype(BF16)
    lo = (v - hi.astype(F32)).astype(BF16)
    return hi, lo


def _hgrn_scan_body(*refs):
    ctx_in, x_in = refs[0:2], refs[2:4]
    nw_ref, ox_ref, oc_ref = refs[4:7]
    v_scr, m_scr, qin_scr, kin_scr, kvt_scr, ste_scr, vec_scr, o_scr = refs[7:]
    lc, lx = ctx_in[0].shape[1], x_in[0].shape[1]
    dk = ox_ref.shape[2]
    c = SCAN_CHUNK
    n_ctx, n_all = lc // c, (lc + lx) // c
    mid_f, mid_b = c // 2 - 1, c // 2

    row = lax.broadcasted_iota(jnp.int32, (c, c), 0)
    col = lax.broadcasted_iota(jnp.int32, (c, c), 1)
    keep_f = col <= row
    keep_b = col >= row
    tri = jnp.where(keep_f, 1.0, 0.0).astype(BF16)

    def chunk_rows(n):
        return pl.ds(pl.multiple_of(n * c, c), c)

    def act(ins, slot, src):
        return ins[0][0, src, slot * dk:(slot + 1) * dk]

    def for_segments(fn, unroll):
        for ins, out_ref, first, count in ((ctx_in, oc_ref, 0, n_ctx), (x_in, ox_ref, n_ctx, n_all - n_ctx)):
            def body(j, carry, ins=ins, out_ref=out_ref, first=first):
                fn(ins, out_ref, j, first + j)
                return carry
            lax.fori_loop(0, count, body, 0, unroll=min(unroll, count))

    def decays(ins, _out, j, n):
        src, rows = chunk_rows(j), chunk_rows(n)
        q = act(ins, HG_Q, src).astype(F32)
        v_scr[rows, :] = act(ins, HG_V, src)
        for lo, r_row, end_row, k_slot in ((0, mid_f, c - 1, HG_KF), (dk, mid_b, 0, HG_KB)):
            lf = ins[1][0, src, lo:lo + dk]
            prefix = _dot(tri, jnp.concatenate(_split2(lf), axis=1))
            b = prefix[:, 0:dk] + prefix[:, dk:2 * dk]
            if lo:
                b = b[c - 1:c, :] - b + lf
            r = b[r_row:r_row + 1, :]
            b_end = b[end_row:end_row + 1, :]
            qin_scr[rows, lo:lo + dk] = (q * jnp.exp(b - r)).astype(BF16)
            kin_scr[rows, lo:lo + dk] = (act(ins, k_slot, src).astype(F32) * jnp.exp(r - b)).astype(BF16)
            vec_scr[n, :, lo:lo + dk] = jnp.concatenate(
                [jnp.exp(r), jnp.exp(b_end - r), jnp.exp(b_end), jnp.zeros((5, dk), F32)], axis=0)

    for_segments(decays, 4)

    def intra(n, carry):
        rows = chunk_rows(n)
        s_f = _dot_nt(qin_scr[rows, 0:dk], kin_scr[rows, 0:dk])
        s_b = _dot_nt(qin_scr[rows, dk:2 * dk], kin_scr[rows, dk:2 * dk])
        m_scr[rows, :] = (jnp.where(keep_f, s_f, 0.0) + jnp.where(keep_b, s_b, 0.0)).astype(BF16)
        kvt_scr[n] = _dot_tn(v_scr[rows, :], kin_scr[rows, :])
        return carry

    lax.fori_loop(0, n_all, intra, 0, unroll=6)

    def carry_state(i, carry):
        st_f, st_b = carry
        cb = jnp.where(i < n_ctx, n_ctx - 1 - i, n_all - 1 - (i - n_ctx))
        vec_f, vec_b = vec_scr[i, :, 0:dk], vec_scr[cb, :, dk:2 * dk]
        ste_scr[i, :, 0:dk] = (st_f * vec_f[0:1, :]).astype(BF16)
        ste_scr[cb, :, dk:2 * dk] = (st_b * vec_b[0:1, :]).astype(BF16)
        st_f = vec_f[2:3, :] * st_f + vec_f[1:2, :] * kvt_scr[i, :, 0:dk]
        st_b = vec_b[2:3, :] * st_b + vec_b[1:2, :] * kvt_scr[cb, :, dk:2 * dk]
        return st_f, st_b

    zero = jnp.zeros((dk, dk), F32)
    lax.fori_loop(0, n_all, carry_state, (zero, zero), unroll=3)

    def readout(n, carry):
        rows = chunk_rows(n)
        o_scr[rows, :] = _dot(m_scr[rows, :], v_scr[rows, :]) + _dot_nt(qin_scr[rows, :], ste_scr[n])
        return carry

    lax.fori_loop(0, n_all, readout, 0, unroll=6)

    def norm_gate(ins, out_ref, j, n):
        src = chunk_rows(j)
        o = o_scr[chunk_rows(n), :]
        o = o * lax.rsqrt(jnp.mean(o * o, axis=-1, keepdims=True) + RMS_EPS) * nw_ref[...]
        g = act(ins, HG_G, src).astype(F32)
        out_ref[0, src, :] = (o * _silu(g)).astype(BF16)

    for_segments(norm_gate, 4)


def _hgrn_scan(ctx_packed, x_packed, norm_w):
    bsz, lx, act_width = x_packed[0].shape
    lc = ctx_packed[0].shape[1]
    d = act_width // HG_ACT_STREAMS
    dk = d // HG_HEADS
    total = lc + lx
    n_all = total // SCAN_CHUNK
    head = lambda length, slots=1: pl.BlockSpec((1, length, slots * dk), lambda b, h: (b, 0, h))
    packed = lambda length: [head(length, HG_ACT_STREAMS), head(length, HG_DIRECTIONS)]
    seq = lambda width: pltpu.VMEM((total, width), BF16)
    return pl.pallas_call(
        _hgrn_scan_body,
        grid=(bsz, HG_HEADS),
        in_specs=packed(lc) + packed(lx) + [_resident((1, dk), lambda b, h: (0, 0))],
        out_specs=[head(lx), head(lc)],
        out_shape=[jax.ShapeDtypeStruct((bsz, lx, d), BF16), jax.ShapeDtypeStruct((bsz, lc, d), BF16)],
        scratch_shapes=[
            seq(dk), seq(dk),
            seq(2 * dk), seq(2 * dk),
            pltpu.VMEM((n_all, dk, 2 * dk), F32),
            pltpu.VMEM((n_all, dk, 2 * dk), BF16),
            pltpu.VMEM((n_all, 8, 2 * dk), F32),
            pltpu.VMEM((total, dk), F32),
        ],
        compiler_params=_params("arbitrary", "arbitrary"),
        name="hgrn2_scan",
    )(*ctx_packed, *x_packed, norm_w.reshape(1, dk))


GELU_C1 = (2.0 / jnp.pi) ** 0.5
GELU_C2 = 0.044715 * GELU_C1


def _gelu_tanh(x):
    half = 0.5 * x
    return half + half * jnp.tanh(x * (GELU_C1 + GELU_C2 * (x * x)))

def _chunk_mlp_body(x_ref, mod_ref, w_in_ref, vg_ref, vb_ref, ws_ref, bs_ref, w_out_ref, g_ref, b_ref, o_ref,
                    *, k, alpha, sub):
    inner = w_out_ref.shape[0]
    groups, chunk, _ = ws_ref.shape
    gw = inner // groups
    pair = 2 if groups % 2 == 0 else 1
    tm, d = x_ref.shape[1:]
    shift, scale, gate = _mod_rows(mod_ref, k)
    for s0 in range(0, tm, sub):
        x = x_ref[0, s0:s0 + sub, :]
        h = (x * (1.0 + scale) + shift).astype(BF16)
        v = _gelu_tanh(_dot(h, w_in_ref[:, inner:2 * inner]))
        v = _layer_norm(v, vg_ref[...], vb_ref[...]).astype(BF16)
        y = None
        for g0 in range(0, groups, pair):
            c0, cw = g0 * gw, pair * gw
            u = _gelu_tanh(_dot(h, w_in_ref[:, c0:c0 + cw]))
            pieces = []
            for r0 in range(0, sub, chunk):
                cols = []
                for gi in range(g0, g0 + pair):
                    sv = _dot(ws_ref[gi], v[r0:r0 + chunk, gi * gw:(gi + 1) * gw]) + bs_ref[gi]
                    cols.append((u[r0:r0 + chunk, (gi - g0) * gw:(gi - g0 + 1) * gw] * sv).astype(BF16))
                pieces.append(jnp.concatenate(cols, axis=1))
            part = _dot(jnp.concatenate(pieces, axis=0), w_out_ref[c0:c0 + cw, 0:d])
            y = part if y is None else y + part
        o_ref[0, s0:s0 + sub, :] = _layer_norm(alpha * x + gate * y, g_ref[...], b_ref[...])


def _chunk_mlp(x, mod, k, w_in, v_g, v_b, w_s, b_s, w_out, g, b, *, alpha, tm=512, sub=256):
    bsz, length, d = x.shape
    inner = w_out.shape[0]
    groups, chunk, _ = w_s.shape
    w_in, w_out = _odd_pitch(w_in), _odd_pitch(w_out)
    return pl.pallas_call(
        functools.partial(_chunk_mlp_body, k=k, alpha=alpha, sub=sub),
        grid=(bsz, length // tm),
        in_specs=[
            pl.BlockSpec((1, tm, d), lambda i, j: (i, j, 0)),
            pl.BlockSpec((1, N_MOD, d), _mod_index(mod)),
            _resident(w_in.shape, lambda i, j: (0, 0)),
            _resident((1, inner), lambda i, j: (0, 0)),
            _resident((1, inner), lambda i, j: (0, 0)),
            _resident((groups, chunk, chunk), lambda i, j: (0, 0, 0)),
            _resident((groups, chunk, 1), lambda i, j: (0, 0, 0)),
            _resident(w_out.shape, lambda i, j: (0, 0)),
            _resident((1, d), lambda i, j: (0, 0)),
            _resident((1, d), lambda i, j: (0, 0)),
        ],
        out_specs=pl.BlockSpec((1, tm, d), lambda i, j: (i, j, 0)),
        out_shape=jax.ShapeDtypeStruct(x.shape, x.dtype),
        compiler_params=_params("arbitrary", "arbitrary"),
        name="chunk_mlp",
    )(x, mod, w_in, v_g.reshape(1, inner), v_b.reshape(1, inner), w_s, b_s.reshape(groups, chunk, 1), w_out,
      g.reshape(1, d), b.reshape(1, d))


def kernel(x, c, ctx, c_ctx, mod_w, mod_b, ln_g, ln_b, ffn_w_in, ffn_w_out, hg_w_in, hg_lower_bounds, hg_norm_w,
           hg_w_out, cm_w_in, cm_v_g, cm_v_b, cm_w_s, cm_b_s, cm_w_out):
    depth = mod_w.shape[0]
    bsz, _, d = x.shape
    alpha = (2 * depth) ** 0.25
    assert CM_CHUNK % GRID_W == 0 and x.shape[1] % CM_CHUNK == 0 and ctx.shape[1] % CM_CHUNK == 0

    pad = (-(bsz + 1)) % 8
    cond = jnp.concatenate([c, c_ctx[None, :], jnp.zeros((pad, d), c.dtype)], axis=0)
    m_all = _modulation(cond, mod_w, mod_b)

    ffn_w_in16, ffn_w_out16 = ffn_w_in.astype(BF16), ffn_w_out.astype(BF16)
    hg_w_in16, hg_w_out16 = hg_w_in.astype(BF16), hg_w_out.astype(BF16)
    lb_raw = hg_lower_bounds.astype(F32)
    cm_w_in16, cm_w_out16, cm_w_s16 = cm_w_in.astype(BF16), cm_w_out.astype(BF16), cm_w_s.astype(BF16)

    ctx_shape = ctx.shape
    flat = lambda t: t.reshape(1, -1, d)
    for i in range(depth):
        last = i == depth - 1
        kind = i % N_MIXERS
        j = i // N_MIXERS
        ctx_needed = (not last) or kind == 0
        mx = m_all[i, :bsz].reshape(bsz, N_MOD, d)
        mc = m_all[i, bsz].reshape(1, N_MOD, d)
        ffn = functools.partial(_ffn, alpha=alpha)
        x = ffn(x, mx, 0, ffn_w_in16[i, 0], ffn_w_out16[i, 0], ln_g[i, 0], ln_b[i, 0])
        if ctx_needed:
            ctx = ffn(flat(ctx), mc, 0, ffn_w_in16[i, 0], ffn_w_out16[i, 0], ln_g[i, 0], ln_b[i, 0]).reshape(ctx_shape)
        if kind == 0:
            x_packed = _hgrn_project(x, mx, hg_w_in16[j], lb_raw, layer=i)
            ctx_packed = [t.reshape(ctx_shape[:2] + t.shape[2:])
                          for t in _hgrn_project(flat(ctx), mc, hg_w_in16[j], lb_raw, layer=i)]
            ox, oc = _hgrn_scan(ctx_packed, x_packed, hg_norm_w[j])
            mixer_x = (ox, hg_w_out16[j], ln_g[i, 1], ln_b[i, 1])
            mixer_c = (flat(oc), hg_w_out16[j], ln_g[i, 1], ln_b[i, 1])
        else:
            mixer_x = mixer_c = None
            cm = functools.partial(_chunk_mlp, alpha=alpha)
            x = cm(x, mx, 1, cm_w_in16[j], cm_v_g[j], cm_v_b[j], cm_w_s16[j], cm_b_s[j], cm_w_out16[j],
                   ln_g[i, 1], ln_b[i, 1])
            if not last:
                ctx = cm(ctx, mc, 1, cm_w_in16[j], cm_v_g[j], cm_v_b[j], cm_w_s16[j], cm_b_s[j], cm_w_out16[j],
                         ln_g[i, 1], ln_b[i, 1])
        x = ffn(x, mx, 2, ffn_w_in16[i, 1], ffn_w_out16[i, 1], ln_g[i, 2], ln_b[i, 2], mixer=mixer_x)
        if not last:
            ctx = ffn(flat(ctx), mc, 2, ffn_w_in16[i, 1], ffn_w_out16[i, 1], ln_g[i, 2], ln_b[i, 2],
                      mixer=mixer_c).reshape(ctx_shape)
    return x
```

```python
import functools

import jax
import jax.numpy as jnp
from jax import lax
from jax.experimental import pallas as pl
from jax.experimental.pallas import tpu as pltpu

GRID_W = 64
N_MIXERS = 2
HG_HEADS = 8
CM_CHUNK = 128
CM_GROUPS = 8
LN_EPS = 1e-5
RMS_EPS = 1e-6
N_MOD = 9

V7X_VMEM_BYTES = 64 * 1024 * 1024
VMEM_LIMIT_BYTES = V7X_VMEM_BYTES - 8 * 1024 * 1024
MXU_DIM = 256
LANES, SUBLANES = 128, 8
SCAN_CHUNK = 128

F32 = jnp.float32
BF16 = jnp.bfloat16


def _params(*semantics):
    return pltpu.CompilerParams(dimension_semantics=semantics, vmem_limit_bytes=VMEM_LIMIT_BYTES)


def _resident(shape, index_map):
    return pl.BlockSpec(shape, index_map, pipeline_mode=pl.Buffered(1))


def _odd_pitch(w):
    tiles = w.shape[1] // LANES
    return jnp.pad(w, ((0, 0), (0, LANES))) if tiles % SUBLANES == 0 else w


def _silu(x):
    half = 0.5 * x
    return half + half * jnp.tanh(half)


def _layer_norm(z, g, b):
    mu = jnp.mean(z, axis=-1, keepdims=True)
    zc = z - mu
    var = jnp.mean(zc * zc, axis=-1, keepdims=True)
    return zc * lax.rsqrt(var + LN_EPS) * g + b


def _mod_rows(mod_ref, k):
    return mod_ref[0, 3 * k:3 * k + 1, :], mod_ref[0, 3 * k + 1:3 * k + 2, :], mod_ref[0, 3 * k + 2:3 * k + 3, :]


def _mod_index(mod):
    return (lambda i, j: (i, 0, 0)) if mod.shape[0] > 1 else (lambda i, j: (0, 0, 0))


def _dot(a, b):
    return jnp.dot(a, b, preferred_element_type=F32)


def _dot_nt(a, b):
    return lax.dot_general(a, b, (((1,), (1,)), ((), ())), preferred_element_type=F32)


def _dot_tn(a, b):
    return lax.dot_general(a, b, (((0,), (0,)), ((), ())), preferred_element_type=F32)


def _modulation_body(cond_ref, w_ref, b_ref, o_ref):
    cond = cond_ref[...]
    act = _silu(cond).astype(BF16)
    o_ref[0] = _dot(act, w_ref[0].astype(BF16)) + b_ref[0]


def _modulation(cond, mod_w, mod_b):
    depth, d, n = mod_w.shape
    r = cond.shape[0]
    tn = d
    return pl.pallas_call(
        _modulation_body,
        grid=(depth, n // tn),
        in_specs=[
            pl.BlockSpec((r, d), lambda i, j: (0, 0)),
            pl.BlockSpec((1, d, tn), lambda i, j: (i, 0, j)),
            pl.BlockSpec((1, 1, tn), lambda i, j: (i, 0, j)),
        ],
        out_specs=pl.BlockSpec((1, r, tn), lambda i, j: (i, 0, j)),
        out_shape=jax.ShapeDtypeStruct((depth, r, n), F32),
        compiler_params=_params("arbitrary", "arbitrary"),
        name="modulation",
    )(cond, mod_w, mod_b.reshape(depth, 1, n))


def _hidden_chunks(hidden):
    step = 4 * MXU_DIM
    return [(c0, min(step, hidden - c0)) for c0 in range(0, hidden, step)]


def _half_step(x, mod_ref, k, w_in_ref, w_out_ref, g_ref, b_ref, alpha):
    hidden = w_out_ref.shape[0]
    shift, scale, gate = _mod_rows(mod_ref, k)
    h = (x * (1.0 + scale) + shift).astype(BF16)
    acc = None
    for c0, cw in _hidden_chunks(hidden):
        gt = _dot(h, w_in_ref[:, c0:c0 + cw])
        up = _dot(h, w_in_ref[:, hidden + c0:hidden + c0 + cw])
        a = (_silu(gt) * up).astype(BF16)
        part = _dot(a, w_out_ref[c0:c0 + cw, :])
        acc = part if acc is None else acc + part
    z = alpha * x + (0.5 * gate) * acc
    return _layer_norm(z, g_ref[...], b_ref[...])


def _ffn_body(x_ref, mod_ref, w_in_ref, w_out_ref, g_ref, b_ref, o_ref, *, k, alpha):
    o_ref[0] = _half_step(x_ref[0], mod_ref, k, w_in_ref, w_out_ref, g_ref, b_ref, alpha)


def _proj_ffn_body(x_ref, y_ref, mod_ref, wp_ref, gp_ref, bp_ref, w_in_ref, w_out_ref, g_ref, b_ref, o_ref,
                   *, k, alpha):
    _s, _c, gate = _mod_rows(mod_ref, k)
    x = _layer_norm(alpha * x_ref[0] + gate * _dot(y_ref[0], wp_ref[...]), gp_ref[...], bp_ref[...])
    o_ref[0] = _half_step(x, mod_ref, k + 1, w_in_ref, w_out_ref, g_ref, b_ref, alpha)


def _ffn(x, mod, k, w_in, w_out, g, b, *, alpha, tm=1024, mixer=None):
    bsz, length, d = x.shape
    hidden = w_out.shape[0]
    tile = pl.BlockSpec((1, tm, d), lambda i, j: (i, j, 0))
    row = _resident((1, d), lambda i, j: (0, 0))
    ffn_specs = [_resident(w_in.shape, lambda i, j: (0, 0)), _resident(w_out.shape, lambda i, j: (0, 0)), row, row]
    ffn_args = (w_in, w_out, g.reshape(1, d), b.reshape(1, d))
    mod_spec = pl.BlockSpec((1, N_MOD, d), _mod_index(mod))
    if mixer is None:
        body = functools.partial(_ffn_body, k=k, alpha=alpha)
        in_specs, args = [tile, mod_spec] + ffn_specs, (x, mod) + ffn_args
    else:
        y, w_proj, g_proj, b_proj = mixer
        kdim = y.shape[2]
        body = functools.partial(_proj_ffn_body, k=k - 1, alpha=alpha)
        in_specs = [tile, pl.BlockSpec((1, tm, kdim), lambda i, j: (i, j, 0)), mod_spec,
                    _resident(w_proj.shape, lambda i, j: (0, 0)), row, row] + ffn_specs
        args = (x, y, mod, w_proj, g_proj.reshape(1, d), b_proj.reshape(1, d)) + ffn_args
    return pl.pallas_call(
        body,
        grid=(bsz, length // tm),
        in_specs=in_specs,
        out_specs=tile,
        out_shape=jax.ShapeDtypeStruct(x.shape, x.dtype),
        compiler_params=_params("arbitrary", "arbitrary"),
        name="swiglu_half_step" if mixer is None else "mixer_proj_swiglu_half_step",
    )(*args)


HG_ACT_STREAMS, HG_Q, HG_V, HG_G, HG_KF, HG_KB = 5, 0, 1, 2, 3, 4
HG_DIRECTIONS = 2


def _put_per_head(ref, slot, n_slots, value):
    dk = value.shape[1] // HG_HEADS
    for h in range(HG_HEADS):
        ref[0, :, (h * n_slots + slot) * dk:(h * n_slots + slot + 1) * dk] = value[:, h * dk:(h + 1) * dk]


def _hgrn_proj_body(x_ref, mod_ref, w_ref, lbraw_ref, act_ref, lf_ref, *, layer):
    d = x_ref.shape[2]
    shift, scale, _gate = _mod_rows(mod_ref, 1)
    h = (x_ref[0] * (1.0 + scale) + shift).astype(BF16)

    def proj(s):
        return _dot(h, w_ref[:, s * d:(s + 1) * d])

    _put_per_head(act_ref, HG_Q, HG_ACT_STREAMS, _silu(proj(0)).astype(BF16))
    _put_per_head(act_ref, HG_V, HG_ACT_STREAMS, proj(1).astype(BF16))
    for direction, k_slot in ((0, HG_KF), (1, HG_KB)):
        raw = lbraw_ref[direction]
        e = jnp.exp(raw - jnp.max(raw, axis=0, keepdims=True))
        lb = jnp.sum(e[:layer + 1, :], axis=0, keepdims=True) / jnp.sum(e, axis=0, keepdims=True)
        half_span = 0.5 * (1.0 - lb)
        swing = half_span * jnp.tanh(0.5 * proj(2 + direction))
        _put_per_head(act_ref, k_slot, HG_ACT_STREAMS, (half_span - swing).astype(BF16))
        _put_per_head(lf_ref, direction, HG_DIRECTIONS, jnp.log((lb + half_span) + swing))
    _put_per_head(act_ref, HG_G, HG_ACT_STREAMS, proj(4).astype(BF16))


def _hgrn_project(x, mod, w_in, lb_raw, *, layer, tm=512):
    bsz, length, d = x.shape
    tile = lambda width: pl.BlockSpec((1, tm, width), lambda i, j: (i, j, 0))
    w_in = _odd_pitch(w_in)
    return pl.pallas_call(
        functools.partial(_hgrn_proj_body, layer=layer),
        grid=(bsz, length // tm),
        in_specs=[
            tile(d),
            pl.BlockSpec((1, N_MOD, d), _mod_index(mod)),
            _resident(w_in.shape, lambda i, j: (0, 0)),
            _resident(lb_raw.shape, lambda i, j: (0, 0, 0)),
        ],
        out_specs=[tile(HG_ACT_STREAMS * d), tile(HG_DIRECTIONS * d)],
        out_shape=[jax.ShapeDtypeStruct((bsz, length, HG_ACT_STREAMS * d), BF16),
                   jax.ShapeDtypeStruct((bsz, length, HG_DIRECTIONS * d), F32)],
        compiler_params=_params("arbitrary", "arbitrary"),
        name="hgrn2_project",
    )(x, mod, w_in, lb_raw)


def _split2(v):
    hi = v.astype(BF16)
    lo = (v - hi.astype(F32)).astype(BF16)
    return hi, lo


def _hgrn_scan_body(*refs):
    ctx_in, x_in = refs[0:2], refs[2:4]
    nw_ref, ox_ref, oc_ref = refs[4:7]
    v_scr, m_scr, qin_scr, kin_scr, kvt_scr, ste_scr, vec_scr, o_scr = refs[7:]
    lc, lx = ctx_in[0].shape[1], x_in[0].shape[1]
    dk = ox_ref.shape[2]
    c = SCAN_CHUNK
    n_ctx, n_all = lc // c, (lc + lx) // c
    mid_f, mid_b = c // 2 - 1, c // 2

    row = lax.broadcasted_iota(jnp.int32, (c, c), 0)
    col = lax.broadcasted_iota(jnp.int32, (c, c), 1)
    keep_f = col <= row
    keep_b = col >= row
    tri = jnp.where(keep_f, 1.0, 0.0).astype(BF16)

    def chunk_rows(n):
        return pl.ds(pl.multiple_of(n * c, c), c)

    def act(ins, slot, src):
        return ins[0][0, src, slot * dk:(slot + 1) * dk]

    def for_segments(fn, unroll):
        for ins, out_ref, first, count in ((ctx_in, oc_ref, 0, n_ctx), (x_in, ox_ref, n_ctx, n_all - n_ctx)):
            def body(j, carry, ins=ins, out_ref=out_ref, first=first):
                fn(ins, out_ref, j, first + j)
                return carry
            lax.fori_loop(0, count, body, 0, unroll=min(unroll, count))

    def decays(ins, _out, j, n):
        src, rows = chunk_rows(j), chunk_rows(n)
        q = act(ins, HG_Q, src).astype(F32)
        v_scr[rows, :] = act(ins, HG_V, src)
        for lo, r_row, end_row, k_slot in ((0, mid_f, c - 1, HG_KF), (dk, mid_b, 0, HG_KB)):
            lf = ins[1][0, src, lo:lo + dk]
            prefix = _dot(tri, jnp.concatenate(_split2(lf), axis=1))
            b = prefix[:, 0:dk] + prefix[:, dk:2 * dk]
            if lo:
                b = b[c - 1:c, :] - b + lf
            r = b[r_row:r_row + 1, :]
            b_end = b[end_row:end_row + 1, :]
            qin_scr[rows, lo:lo + dk] = (q * jnp.exp(b - r)).astype(BF16)
            kin_scr[rows, lo:lo + dk] = (act(ins, k_slot, src).astype(F32) * jnp.exp(r - b)).astype(BF16)
            vec_scr[n, :, lo:lo + dk] = jnp.concatenate(
                [jnp.exp(r), jnp.exp(b_end - r), jnp.exp(b_end), jnp.zeros((5, dk), F32)], axis=0)

    for_segments(decays, 4)

    def intra(n, carry):
        rows = chunk_rows(n)
        s_f = _dot_nt(qin_scr[rows, 0:dk], kin_scr[rows, 0:dk])
        s_b = _dot_nt(qin_scr[rows, dk:2 * dk], kin_scr[rows, dk:2 * dk])
        m_scr[rows, :] = (jnp.where(keep_f, s_f, 0.0) + jnp.where(keep_b, s_b, 0.0)).astype(BF16)
        kvt_scr[n] = _dot_tn(v_scr[rows, :], kin_scr[rows, :])
        return carry

    lax.fori_loop(0, n_all, intra, 0, unroll=6)

    def carry_state(i, carry):
        st_f, st_b = carry
        cb = jnp.where(i < n_ctx, n_ctx - 1 - i, n_all - 1 - (i - n_ctx))
        vec_f, vec_b = vec_scr[i, :, 0:dk], vec_scr[cb, :, dk:2 * dk]
        ste_scr[i, :, 0:dk] = (st_f * vec_f[0:1, :]).astype(BF16)
        ste_scr[cb, :, dk:2 * dk] = (st_b * vec_b[0:1, :]).astype(BF16)
        st_f = vec_f[2:3, :] * st_f + vec_f[1:2, :] * kvt_scr[i, :, 0:dk]
        st_b = vec_b[2:3, :] * st_b + vec_b[1:2, :] * kvt_scr[cb, :, dk:2 * dk]
        return st_f, st_b

    zero = jnp.zeros((dk, dk), F32)
    lax.fori_loop(0, n_all, carry_state, (zero, zero), unroll=3)

    def readout(n, carry):
        rows = chunk_rows(n)
        o_scr[rows, :] = _dot(m_scr[rows, :], v_scr[rows, :]) + _dot_nt(qin_scr[rows, :], ste_scr[n])
        return carry

    lax.fori_loop(0, n_all, readout, 0, unroll=6)

    def norm_gate(ins, out_ref, j, n):
        src = chunk_rows(j)
        o = o_scr[chunk_rows(n), :]
        o = o * lax.rsqrt(jnp.mean(o * o, axis=-1, keepdims=True) + RMS_EPS) * nw_ref[...]
        g = act(ins, HG_G, src).astype(F32)
        out_ref[0, src, :] = (o * _silu(g)).astype(BF16)

    for_segments(norm_gate, 4)


def _hgrn_scan(ctx_packed, x_packed, norm_w):
    bsz, lx, act_width = x_packed[0].shape
    lc = ctx_packed[0].shape[1]
    d = act_width // HG_ACT_STREAMS
    dk = d // HG_HEADS
    total = lc + lx
    n_all = total // SCAN_CHUNK
    head = lambda length, slots=1: pl.BlockSpec((1, length, slots * dk), lambda b, h: (b, 0, h))
    packed = lambda length: [head(length, HG_ACT_STREAMS), head(length, HG_DIRECTIONS)]
    seq = lambda width: pltpu.VMEM((total, width), BF16)
    return pl.pallas_call(
        _hgrn_scan_body,
        grid=(bsz, HG_HEADS),
        in_specs=packed(lc) + packed(lx) + [_resident((1, dk), lambda b, h: (0, 0))],
        out_specs=[head(lx), head(lc)],
        out_shape=[jax.ShapeDtypeStruct((bsz, lx, d), BF16), jax.ShapeDtypeStruct((bsz, lc, d), BF16)],
        scratch_shapes=[
            seq(dk), seq(dk),
            seq(2 * dk), seq(2 * dk),
            pltpu.VMEM((n_all, dk, 2 * dk), F32),
            pltpu.VMEM((n_all, dk, 2 * dk), BF16),
            pltpu.VMEM((n_all, 8, 2 * dk), F32),
            pltpu.VMEM((total, dk), F32),
        ],
        compiler_params=_params("arbitrary", "arbitrary"),
        name="hgrn2_scan",
    )(*ctx_packed, *x_packed, norm_w.reshape(1, dk))


GELU_C1 = (2.0 / jnp.pi) ** 0.5
GELU_C2 = 0.044715 * GELU_C1


def _gelu_tanh(x):
    half = 0.5 * x
    return half + half * jnp.tanh(x * (GELU_C1 + GELU_C2 * (x * x)))

def _chunk_mlp_body(x_ref, mod_ref, w_in_ref, vg_ref, vb_ref, ws_ref, bs_ref, w_out_ref, g_ref, b_ref, o_ref,
                    *, k, alpha, sub):
    inner = w_out_ref.shape[0]
    groups, chunk, _ = ws_ref.shape
    gw = inner // groups
    pair = 2 if groups % 2 == 0 else 1
    tm, d = x_ref.shape[1:]
    shift, scale, gate = _mod_rows(mod_ref, k)
    for s0 in range(0, tm, sub):
        x = x_ref[0, s0:s0 + sub, :]
        h = (x * (1.0 + scale) + shift).astype(BF16)
        v = _gelu_tanh(_dot(h, w_in_ref[:, inner:2 * inner]))
        v = _layer_norm(v, vg_ref[...], vb_ref[...]).astype(BF16)
        y = None
        for g0 in range(0, groups, pair):
            c0, cw = g0 * gw, pair * gw
            u = _gelu_tanh(_dot(h, w_in_ref[:, c0:c0 + cw]))
            pieces = []
            for r0 in range(0, sub, chunk):
                cols = []
                for gi in range(g0, g0 + pair):
                    sv = _dot(ws_ref[gi], v[r0:r0 + chunk, gi * gw:(gi + 1) * gw]) + bs_ref[gi]
                    cols.append((u[r0:r0 + chunk, (gi - g0) * gw:(gi - g0 + 1) * gw] * sv).astype(BF16))
                pieces.append(jnp.concatenate(cols, axis=1))
            part = _dot(jnp.concatenate(pieces, axis=0), w_out_ref[c0:c0 + cw, 0:d])
            y = part if y is None else y + part
        o_ref[0, s0:s0 + sub, :] = _layer_norm(alpha * x + gate * y, g_ref[...], b_ref[...])


def _chunk_mlp(x, mod, k, w_in, v_g, v_b, w_s, b_s, w_out, g, b, *, alpha, tm=1024, sub=256):
    bsz, length, d = x.shape
    inner = w_out.shape[0]
    groups, chunk, _ = w_s.shape
    tm = min(tm, length)
    w_in, w_out = _odd_pitch(w_in), _odd_pitch(w_out)
    return pl.pallas_call(
        functools.partial(_chunk_mlp_body, k=k, alpha=alpha, sub=sub),
        grid=(bsz, length // tm),
        in_specs=[
            pl.BlockSpec((1, tm, d), lambda i, j: (i, j, 0)),
            pl.BlockSpec((1, N_MOD, d), _mod_index(mod)),
            _resident(w_in.shape, lambda i, j: (0, 0)),
            _resident((1, inner), lambda i, j: (0, 0)),
            _resident((1, inner), lambda i, j: (0, 0)),
            _resident((groups, chunk, chunk), lambda i, j: (0, 0, 0)),
            _resident((groups, chunk, 1), lambda i, j: (0, 0, 0)),
            _resident(w_out.shape, lambda i, j: (0, 0)),
            _resident((1, d), lambda i, j: (0, 0)),
            _resident((1, d), lambda i, j: (0, 0)),
        ],
        out_specs=pl.BlockSpec((1, tm, d), lambda i, j: (i, j, 0)),
        out_shape=jax.ShapeDtypeStruct(x.shape, x.dtype),
        compiler_params=_params("arbitrary", "arbitrary"),
        name="chunk_mlp",
    )(x, mod, w_in, v_g.reshape(1, inner), v_b.reshape(1, inner), w_s, b_s.reshape(groups, chunk, 1), w_out,
      g.reshape(1, d), b.reshape(1, d))


def kernel(x, c, ctx, c_ctx, mod_w, mod_b, ln_g, ln_b, ffn_w_in, ffn_w_out, hg_w_in, hg_lower_bounds, hg_norm_w,
           hg_w_out, cm_w_in, cm_v_g, cm_v_b, cm_w_s, cm_b_s, cm_w_out):
    depth = mod_w.shape[0]
    bsz, _, d = x.shape
    alpha = (2 * depth) ** 0.25
    assert CM_CHUNK % GRID_W == 0 and x.shape[1] % CM_CHUNK == 0 and ctx.shape[1] % CM_CHUNK == 0

    pad = (-(bsz + 1)) % 8
    cond = jnp.concatenate([c, c_ctx[None, :], jnp.zeros((pad, d), c.dtype)], axis=0)
    m_all = _modulation(cond, mod_w, mod_b)

    lb_raw = hg_lower_bounds.astype(F32)
    bf16 = lambda w: w.astype(BF16)

    ctx_shape = ctx.shape
    flat = lambda t: t.reshape(1, -1, d)
    for i in range(depth):
        last = i == depth - 1
        kind = i % N_MIXERS
        j = i // N_MIXERS
        ctx_needed = (not last) or kind == 0
        mx = m_all[i, :bsz].reshape(bsz, N_MOD, d)
        mc = m_all[i, bsz].reshape(1, N_MOD, d)
        ffn = functools.partial(_ffn, alpha=alpha)
        ffn0 = (bf16(ffn_w_in[i, 0]), bf16(ffn_w_out[i, 0]), ln_g[i, 0], ln_b[i, 0])
        ffn1 = (bf16(ffn_w_in[i, 1]), bf16(ffn_w_out[i, 1]), ln_g[i, 2], ln_b[i, 2])
        x = ffn(x, mx, 0, *ffn0)
        if ctx_needed:
            ctx = ffn(flat(ctx), mc, 0, *ffn0).reshape(ctx_shape)
        if kind == 0:
            hg_w_in16, hg_w_out16 = bf16(hg_w_in[j]), bf16(hg_w_out[j])
            x_packed = _hgrn_project(x, mx, hg_w_in16, lb_raw, layer=i)
            ctx_packed = [t.reshape(ctx_shape[:2] + t.shape[2:])
                          for t in _hgrn_project(flat(ctx), mc, hg_w_in16, lb_raw, layer=i)]
            ox, oc = _hgrn_scan(ctx_packed, x_packed, hg_norm_w[j])
            mixer_x = (ox, hg_w_out16, ln_g[i, 1], ln_b[i, 1])
            mixer_c = (flat(oc), hg_w_out16, ln_g[i, 1], ln_b[i, 1])
        else:
            mixer_x = mixer_c = None
            cm = functools.partial(_chunk_mlp, alpha=alpha)
            cm_args = (bf16(cm_w_in[j]), cm_v_g[j], cm_v_b[j], bf16(cm_w_s[j]), cm_b_s[j], bf16(cm_w_out[j]),
                       ln_g[i, 1], ln_b[i, 1])
            x = cm(x, mx, 1, *cm_args)
            if not last:
                ctx = cm(ctx, mc, 1, *cm_args)
        x = ffn(x, mx, 2, *ffn1, mixer=mixer_x)
        if not last:
            ctx = ffn(flat(ctx), mc, 2, *ffn1, mixer=mixer_c).reshape(ctx_shape)
    return x
```

```python
import functools

import jax
import jax.numpy as jnp
from jax import lax
from jax.experimental import pallas as pl
from jax.experimental.pallas import tpu as pltpu

GRID_W = 64
N_MIXERS = 2
HG_HEADS = 8
CM_CHUNK = 128
CM_GROUPS = 8
LN_EPS = 1e-5
RMS_EPS = 1e-6
N_MOD = 9

V7X_VMEM_BYTES = 64 * 1024 * 1024
VMEM_LIMIT_BYTES = V7X_VMEM_BYTES - 8 * 1024 * 1024
MXU_DIM = 256
LANES, SUBLANES = 128, 8
SCAN_CHUNK = 128

F32 = jnp.float32
BF16 = jnp.bfloat16


def _params(*semantics):
    return pltpu.CompilerParams(dimension_semantics=semantics, vmem_limit_bytes=VMEM_LIMIT_BYTES)


def _resident(shape, index_map):
    return pl.BlockSpec(shape, index_map, pipeline_mode=pl.Buffered(1))


def _odd_pitch(w):
    tiles = w.shape[1] // LANES
    return jnp.pad(w, ((0, 0), (0, LANES))) if tiles % SUBLANES == 0 else w


def _silu(x):
    half = 0.5 * x
    return half + half * jnp.tanh(half)


def _layer_norm(z, g, b):
    mu = jnp.mean(z, axis=-1, keepdims=True)
    zc = z - mu
    var = jnp.mean(zc * zc, axis=-1, keepdims=True)
    return zc * lax.rsqrt(var + LN_EPS) * g + b


def _mod_rows(mod_ref, k):
    return mod_ref[0, 3 * k:3 * k + 1, :], mod_ref[0, 3 * k + 1:3 * k + 2, :], mod_ref[0, 3 * k + 2:3 * k + 3, :]


def _mod_index(mod):
    return (lambda i, j: (i, 0, 0)) if mod.shape[0] > 1 else (lambda i, j: (0, 0, 0))


def _dot(a, b):
    return jnp.dot(a, b, preferred_element_type=F32)


def _dot_nt(a, b):
    return lax.dot_general(a, b, (((1,), (1,)), ((), ())), preferred_element_type=F32)


def _dot_tn(a, b):
    return lax.dot_general(a, b, (((0,), (0,)), ((), ())), preferred_element_type=F32)


def _modulation_body(cond_ref, w_ref, b_ref, o_ref):
    cond = cond_ref[...]
    act = _silu(cond).astype(BF16)
    o_ref[0] = _dot(act, w_ref[0].astype(BF16)) + b_ref[0]


def _modulation(cond, mod_w, mod_b):
    depth, d, n = mod_w.shape
    r = cond.shape[0]
    tn = d
    return pl.pallas_call(
        _modulation_body,
        grid=(depth, n // tn),
        in_specs=[
            pl.BlockSpec((r, d), lambda i, j: (0, 0)),
            pl.BlockSpec((1, d, tn), lambda i, j: (i, 0, j)),
            pl.BlockSpec((1, 1, tn), lambda i, j: (i, 0, j)),
        ],
        out_specs=pl.BlockSpec((1, r, tn), lambda i, j: (i, 0, j)),
        out_shape=jax.ShapeDtypeStruct((depth, r, n), F32),
        compiler_params=_params("arbitrary", "arbitrary"),
        name="modulation",
    )(cond, mod_w, mod_b.reshape(depth, 1, n))


def _hidden_chunks(hidden):
    step = 4 * MXU_DIM
    return [(c0, min(step, hidden - c0)) for c0 in range(0, hidden, step)]


def _half_step(x, mod_ref, k, w_in_ref, w_out_ref, g_ref, b_ref, alpha):
    hidden = w_out_ref.shape[0]
    shift, scale, gate = _mod_rows(mod_ref, k)
    h = (x * (1.0 + scale) + shift).astype(BF16)
    acc = None
    for c0, cw in _hidden_chunks(hidden):
        gt = _dot(h, w_in_ref[:, c0:c0 + cw])
        up = _dot(h, w_in_ref[:, hidden + c0:hidden + c0 + cw])
        a = (_silu(gt) * up).astype(BF16)
        part = _dot(a, w_out_ref[c0:c0 + cw, :])
        acc = part if acc is None else acc + part
    z = alpha * x + gate * (0.5 * acc)
    return _layer_norm(z, g_ref[...], b_ref[...])


def _ffn_body(x_ref, mod_ref, w_in_ref, w_out_ref, g_ref, b_ref, o_ref, *, k, alpha):
    o_ref[0] = _half_step(x_ref[0], mod_ref, k, w_in_ref, w_out_ref, g_ref, b_ref, alpha)


def _proj_ffn_body(x_ref, y_ref, mod_ref, wp_ref, gp_ref, bp_ref, w_in_ref, w_out_ref, g_ref, b_ref, o_ref,
                   *, k, alpha):
    _s, _c, gate = _mod_rows(mod_ref, k)
    x = _layer_norm(alpha * x_ref[0] + gate * _dot(y_ref[0], wp_ref[...]), gp_ref[...], bp_ref[...])
    o_ref[0] = _half_step(x, mod_ref, k + 1, w_in_ref, w_out_ref, g_ref, b_ref, alpha)


def _ffn(x, mod, k, w_in, w_out, g, b, *, alpha, tm=1024, mixer=None):
    bsz, length, d = x.shape
    hidden = w_out.shape[0]
    tile = pl.BlockSpec((1, tm, d), lambda i, j: (i, j, 0))
    row = _resident((1, d), lambda i, j: (0, 0))
    ffn_specs = [_resident(w_in.shape, lambda i, j: (0, 0)), _resident(w_out.shape, lambda i, j: (0, 0)), row, row]
    ffn_args = (w_in, w_out, g.reshape(1, d), b.reshape(1, d))
    mod_spec = pl.BlockSpec((1, N_MOD, d), _mod_index(mod))
    if mixer is None:
        body = functools.partial(_ffn_body, k=k, alpha=alpha)
        in_specs, args = [tile, mod_spec] + ffn_specs, (x, mod) + ffn_args
    else:
        y, w_proj, g_proj, b_proj = mixer
        kdim = y.shape[2]
        body = functools.partial(_proj_ffn_body, k=k - 1, alpha=alpha)
        in_specs = [tile, pl.BlockSpec((1, tm, kdim), lambda i, j: (i, j, 0)), mod_spec,
                    _resident(w_proj.shape, lambda i, j: (0, 0)), row, row] + ffn_specs
        args = (x, y, mod, w_proj, g_proj.reshape(1, d), b_proj.reshape(1, d)) + ffn_args
    return pl.pallas_call(
        body,
        grid=(bsz, length // tm),
        in_specs=in_specs,
        out_specs=tile,
        out_shape=jax.ShapeDtypeStruct(x.shape, x.dtype),
        compiler_params=_params("arbitrary", "arbitrary"),
        name="swiglu_half_step" if mixer is None else "mixer_proj_swiglu_half_step",
    )(*args)


HG_STREAMS = (BF16, BF16, BF16, BF16, BF16, F32, F32)


def _hgrn_proj_body(x_ref, mod_ref, w_ref, lbraw_ref, q_ref, v_ref, g_ref, kf_ref, kb_ref, lff_ref, lfb_ref,
                    *, layer):
    d = x_ref.shape[2]
    shift, scale, _gate = _mod_rows(mod_ref, 1)
    h = (x_ref[0] * (1.0 + scale) + shift).astype(BF16)

    def proj(s):
        return _dot(h, w_ref[:, s * d:(s + 1) * d])

    pq = proj(0)
    q_ref[0] = _silu(pq).astype(BF16)
    v_ref[0] = proj(1).astype(BF16)
    for direction, k_ref, lf_ref in ((0, kf_ref, lff_ref), (1, kb_ref, lfb_ref)):
        raw = lbraw_ref[direction]
        e = jnp.exp(raw - jnp.max(raw, axis=0, keepdims=True))
        lb = jnp.sum(e[:layer + 1, :], axis=0, keepdims=True) / jnp.sum(e, axis=0, keepdims=True)
        half_span = 0.5 * (1.0 - lb)
        swing = half_span * jnp.tanh(0.5 * proj(2 + direction))
        k_ref[0] = (half_span - swing).astype(BF16)
        lf_ref[0] = jnp.log((lb + half_span) + swing)
    g_ref[0] = proj(4).astype(BF16)


def _hgrn_project(x, mod, w_in, lb_raw, *, layer, tm=512):
    bsz, length, d = x.shape
    tile = pl.BlockSpec((1, tm, d), lambda i, j: (i, j, 0))
    return pl.pallas_call(
        functools.partial(_hgrn_proj_body, layer=layer),
        grid=(bsz, length // tm),
        in_specs=[
            tile,
            pl.BlockSpec((1, N_MOD, d), _mod_index(mod)),
            _resident(w_in.shape, lambda i, j: (0, 0)),
            _resident(lb_raw.shape, lambda i, j: (0, 0, 0)),
        ],
        out_specs=[tile] * len(HG_STREAMS),
        out_shape=[jax.ShapeDtypeStruct(x.shape, dt) for dt in HG_STREAMS],
        compiler_params=_params("arbitrary", "arbitrary"),
        name="hgrn2_project",
    )(x, mod, w_in, lb_raw)


def _split2(v):
    hi = v.astype(BF16)
    lo = (v - hi.astype(F32)).astype(BF16)
    return hi, lo


def _hgrn_scan_body(*refs):
    ns = len(HG_STREAMS)
    ctx_in, x_in = refs[0:ns], refs[ns:2 * ns]
    nw_ref, ox_ref, oc_ref = refs[2 * ns:2 * ns + 3]
    v_scr, m_scr, qin_scr, kin_scr, kvt_scr, ste_scr, vec_scr, o_scr = refs[2 * ns + 3:]
    lc, lx = ctx_in[0].shape[1], x_in[0].shape[1]
    dk = x_in[0].shape[2]
    c = SCAN_CHUNK
    n_ctx, n_all = lc // c, (lc + lx) // c
    mid_f, mid_b = c // 2 - 1, c // 2

    row = lax.broadcasted_iota(jnp.int32, (c, c), 0)
    col = lax.broadcasted_iota(jnp.int32, (c, c), 1)
    keep_f = col <= row
    keep_b = col >= row
    tri = jnp.where(keep_f, 1.0, 0.0).astype(BF16)

    def chunk_rows(n):
        return pl.ds(pl.multiple_of(n * c, c), c)

    def for_segments(fn, unroll):
        for ins, out_ref, first, count in ((ctx_in, oc_ref, 0, n_ctx), (x_in, ox_ref, n_ctx, n_all - n_ctx)):
            def body(j, carry, ins=ins, out_ref=out_ref, first=first):
                fn(ins, out_ref, j, first + j)
                return carry
            lax.fori_loop(0, count, body, 0, unroll=min(unroll, count))

    def decays(ins, _out, j, n):
        q_ref, v_ref, _g, kf_ref, kb_ref, lff_ref, lfb_ref = ins
        src, rows = chunk_rows(j), chunk_rows(n)
        q = q_ref[0, src, :].astype(F32)
        v_scr[rows, :] = v_ref[0, src, :]
        for lo, lf_ref, r_row, end_row, k_ref in ((0, lff_ref, mid_f, c - 1, kf_ref), (dk, lfb_ref, mid_b, 0, kb_ref)):
            lf = lf_ref[0, src, :]
            prefix = _dot(tri, jnp.concatenate(_split2(lf), axis=1))
            b = prefix[:, 0:dk] + prefix[:, dk:2 * dk]
            if lo:
                b = b[c - 1:c, :] - b + lf
            r = b[r_row:r_row + 1, :]
            b_end = b[end_row:end_row + 1, :]
            qin_scr[rows, lo:lo + dk] = (q * jnp.exp(b - r)).astype(BF16)
            kin_scr[rows, lo:lo + dk] = (k_ref[0, src, :].astype(F32) * jnp.exp(r - b)).astype(BF16)
            vec_scr[n, :, lo:lo + dk] = jnp.concatenate(
                [jnp.exp(r), jnp.exp(b_end - r), jnp.exp(b_end), jnp.zeros((5, dk), F32)], axis=0)

    for_segments(decays, 4)

    def intra(n, carry):
        rows = chunk_rows(n)
        s_f = _dot_nt(qin_scr[rows, 0:dk], kin_scr[rows, 0:dk])
        s_b = _dot_nt(qin_scr[rows, dk:2 * dk], kin_scr[rows, dk:2 * dk])
        m_scr[rows, :] = (jnp.where(keep_f, s_f, 0.0) + jnp.where(keep_b, s_b, 0.0)).astype(BF16)
        kvt_scr[n] = _dot_tn(v_scr[rows, :], kin_scr[rows, :])
        return carry

    lax.fori_loop(0, n_all, intra, 0, unroll=True)

    def carry_state(i, carry):
        st_f, st_b = carry
        cb = jnp.where(i < n_ctx, n_ctx - 1 - i, n_all - 1 - (i - n_ctx))
        vec_f, vec_b = vec_scr[i, :, 0:dk], vec_scr[cb, :, dk:2 * dk]
        ste_scr[i, :, 0:dk] = (st_f * vec_f[0:1, :]).astype(BF16)
        ste_scr[cb, :, dk:2 * dk] = (st_b * vec_b[0:1, :]).astype(BF16)
        st_f = vec_f[2:3, :] * st_f + vec_f[1:2, :] * kvt_scr[i, :, 0:dk]
        st_b = vec_b[2:3, :] * st_b + vec_b[1:2, :] * kvt_scr[cb, :, dk:2 * dk]
        return st_f, st_b

    zero = jnp.zeros((dk, dk), F32)
    lax.fori_loop(0, n_all, carry_state, (zero, zero), unroll=6)

    def readout(n, carry):
        rows = chunk_rows(n)
        o_scr[rows, :] = _dot(m_scr[rows, :], v_scr[rows, :]) + _dot_nt(qin_scr[rows, :], ste_scr[n])
        return carry

    lax.fori_loop(0, n_all, readout, 0, unroll=True)

    def norm_gate(ins, out_ref, j, n):
        src = chunk_rows(j)
        o = o_scr[chunk_rows(n), :]
        o = o * lax.rsqrt(jnp.mean(o * o, axis=-1, keepdims=True) + RMS_EPS) * nw_ref[...]
        g = ins[2][0, src, :].astype(F32)
        out_ref[0, src, :] = (o * _silu(g)).astype(BF16)

    for_segments(norm_gate, 4)


def _hgrn_scan(ctx_streams, x_streams, norm_w):
    bsz, lx, d = x_streams[0].shape
    lc = ctx_streams[0].shape[1]
    dk = d // HG_HEADS
    total = lc + lx
    n_all = total // SCAN_CHUNK
    head = lambda length: pl.BlockSpec((1, length, dk), lambda b, h: (b, 0, h))
    seq = lambda width: pltpu.VMEM((total, width), BF16)
    return pl.pallas_call(
        _hgrn_scan_body,
        grid=(bsz, HG_HEADS),
        in_specs=[head(lc)] * len(ctx_streams) + [head(lx)] * len(x_streams) + [_resident((1, dk), lambda b, h: (0, 0))],
        out_specs=[head(lx), head(lc)],
        out_shape=[jax.ShapeDtypeStruct((bsz, lx, d), BF16), jax.ShapeDtypeStruct((bsz, lc, d), BF16)],
        scratch_shapes=[
            seq(dk), seq(dk),
            seq(2 * dk), seq(2 * dk),
            pltpu.VMEM((n_all, dk, 2 * dk), F32),
            pltpu.VMEM((n_all, dk, 2 * dk), BF16),
            pltpu.VMEM((n_all, 8, 2 * dk), F32),
            pltpu.VMEM((total, dk), F32),
        ],
        compiler_params=_params("arbitrary", "arbitrary"),
        name="hgrn2_scan",
    )(*ctx_streams, *x_streams, norm_w.reshape(1, dk))


GELU_C1 = (2.0 / jnp.pi) ** 0.5
GELU_C2 = 0.044715 * GELU_C1


def _gelu_tanh(x):
    half = 0.5 * x
    return half + half * jnp.tanh(x * (GELU_C1 + GELU_C2 * (x * x)))

def _chunk_mlp_body(x_ref, mod_ref, w_in_ref, vg_ref, vb_ref, ws_ref, bs_ref, w_out_ref, g_ref, b_ref, o_ref,
                    *, k, alpha, sub):
    inner = w_out_ref.shape[0]
    groups, chunk, _ = ws_ref.shape
    gw = inner // groups
    pair = 2 if groups % 2 == 0 else 1
    tm, d = x_ref.shape[1:]
    shift, scale, gate = _mod_rows(mod_ref, k)
    for s0 in range(0, tm, sub):
        x = x_ref[0, s0:s0 + sub, :]
        h = (x * (1.0 + scale) + shift).astype(BF16)
        v = _gelu_tanh(_dot(h, w_in_ref[:, inner:2 * inner]))
        v = _layer_norm(v, vg_ref[...], vb_ref[...]).astype(BF16)
        y = None
        for g0 in range(0, groups, pair):
            c0, cw = g0 * gw, pair * gw
            u = _gelu_tanh(_dot(h, w_in_ref[:, c0:c0 + cw]))
            pieces = []
            for r0 in range(0, sub, chunk):
                cols = []
                for gi in range(g0, g0 + pair):
                    sv = _dot(ws_ref[gi], v[r0:r0 + chunk, gi * gw:(gi + 1) * gw]) + bs_ref[gi]
                    cols.append((u[r0:r0 + chunk, (gi - g0) * gw:(gi - g0 + 1) * gw] * sv).astype(BF16))
                pieces.append(jnp.concatenate(cols, axis=1))
            part = _dot(jnp.concatenate(pieces, axis=0), w_out_ref[c0:c0 + cw, 0:d])
            y = part if y is None else y + part
        o_ref[0, s0:s0 + sub, :] = _layer_norm(alpha * x + gate * y, g_ref[...], b_ref[...])


def _chunk_mlp(x, mod, k, w_in, v_g, v_b, w_s, b_s, w_out, g, b, *, alpha, tm=512, sub=256):
    bsz, length, d = x.shape
    inner = w_out.shape[0]
    groups, chunk, _ = w_s.shape
    w_in, w_out = _odd_pitch(w_in), _odd_pitch(w_out)
    return pl.pallas_call(
        functools.partial(_chunk_mlp_body, k=k, alpha=alpha, sub=sub),
        grid=(bsz, length // tm),
        in_specs=[
            pl.BlockSpec((1, tm, d), lambda i, j: (i, j, 0)),
            pl.BlockSpec((1, N_MOD, d), _mod_index(mod)),
            _resident(w_in.shape, lambda i, j: (0, 0)),
            _resident((1, inner), lambda i, j: (0, 0)),
            _resident((1, inner), lambda i, j: (0, 0)),
            _resident((groups, chunk, chunk), lambda i, j: (0, 0, 0)),
            _resident((groups, chunk, 1), lambda i, j: (0, 0, 0)),
            _resident(w_out.shape, lambda i, j: (0, 0)),
            _resident((1, d), lambda i, j: (0, 0)),
            _resident((1, d), lambda i, j: (0, 0)),
        ],
        out_specs=pl.BlockSpec((1, tm, d), lambda i, j: (i, j, 0)),
        out_shape=jax.ShapeDtypeStruct(x.shape, x.dtype),
        compiler_params=_params("arbitrary", "arbitrary"),
        name="chunk_mlp",
    )(x, mod, w_in, v_g.reshape(1, inner), v_b.reshape(1, inner), w_s, b_s.reshape(groups, chunk, 1), w_out,
      g.reshape(1, d), b.reshape(1, d))


def kernel(x, c, ctx, c_ctx, mod_w, mod_b, ln_g, ln_b, ffn_w_in, ffn_w_out, hg_w_in, hg_lower_bounds, hg_norm_w,
           hg_w_out, cm_w_in, cm_v_g, cm_v_b, cm_w_s, cm_b_s, cm_w_out):
    depth = mod_w.shape[0]
    bsz, _, d = x.shape
    alpha = (2 * depth) ** 0.25
    assert CM_CHUNK % GRID_W == 0 and x.shape[1] % CM_CHUNK == 0 and ctx.shape[1] % CM_CHUNK == 0

    pad = (-(bsz + 1)) % 8
    cond = jnp.concatenate([c, c_ctx[None, :], jnp.zeros((pad, d), c.dtype)], axis=0)
    m_all = _modulation(cond, mod_w, mod_b)

    ffn_w_in16, ffn_w_out16 = ffn_w_in.astype(BF16), ffn_w_out.astype(BF16)
    hg_w_in16, hg_w_out16 = hg_w_in.astype(BF16), hg_w_out.astype(BF16)
    lb_raw = hg_lower_bounds.astype(F32)
    cm_w_in16, cm_w_out16, cm_w_s16 = cm_w_in.astype(BF16), cm_w_out.astype(BF16), cm_w_s.astype(BF16)

    ctx_shape = ctx.shape
    flat = lambda t: t.reshape(1, -1, d)
    for i in range(depth):
        last = i == depth - 1
        kind = i % N_MIXERS
        j = i // N_MIXERS
        ctx_needed = (not last) or kind == 0
        mx = m_all[i, :bsz].reshape(bsz, N_MOD, d)
        mc = m_all[i, bsz].reshape(1, N_MOD, d)
        ffn = functools.partial(_ffn, alpha=alpha)
        x = ffn(x, mx, 0, ffn_w_in16[i, 0], ffn_w_out16[i, 0], ln_g[i, 0], ln_b[i, 0])
        if ctx_needed:
            ctx = ffn(flat(ctx), mc, 0, ffn_w_in16[i, 0], ffn_w_out16[i, 0], ln_g[i, 0], ln_b[i, 0]).reshape(ctx_shape)
        if kind == 0:
            x_streams = _hgrn_project(x, mx, hg_w_in16[j], lb_raw, layer=i)
            ctx_streams = [s.reshape(ctx_shape) for s in _hgrn_project(flat(ctx), mc, hg_w_in16[j], lb_raw, layer=i)]
            ox, oc = _hgrn_scan(ctx_streams, x_streams, hg_norm_w[j])
            mixer_x = (ox, hg_w_out16[j], ln_g[i, 1], ln_b[i, 1])
            mixer_c = (flat(oc), hg_w_out16[j], ln_g[i, 1], ln_b[i, 1])
        else:
            mixer_x = mixer_c = None
            cm = functools.partial(_chunk_mlp, alpha=alpha)
            x = cm(x, mx, 1, cm_w_in16[j], cm_v_g[j], cm_v_b[j], cm_w_s16[j], cm_b_s[j], cm_w_out16[j],
                   ln_g[i, 1], ln_b[i, 1])
            if not last:
                ctx = cm(ctx, mc, 1, cm_w_in16[j], cm_v_g[j], cm_v_b[j], cm_w_s16[j], cm_b_s[j], cm_w_out16[j],
                         ln_g[i, 1], ln_b[i, 1])
        x = ffn(x, mx, 2, ffn_w_in16[i, 1], ffn_w_out16[i, 1], ln_g[i, 2], ln_b[i, 2], mixer=mixer_x)
        if not last:
            ctx = ffn(flat(ctx), mc, 2, ffn_w_in16[i, 1], ffn_w_out16[i, 1], ln_g[i, 2], ln_b[i, 2],
                      mixer=mixer_c).reshape(ctx_shape)
    return x
```
